```python
import jax, jax.numpy as jnp
from jax import lax
import numpy as np

D_MODEL = 2048
BATCH = 1
SEQ = 8192
DEPTH = 1
DEC_BATCH = 32
DEC_SEQ = 1
PAST_LEN = 16384
PAGE_SIZE = 128

N_HEADS = 8
HEAD_DIM = 128
ATTN_WIDTH = N_HEADS * HEAD_DIM
N_IDX_HEADS = 16
IDX_DIM = 64
TOPK_MAX = 256
LRU_WIDTH = 1024
N_LRU_BLOCKS = 8
LRU_BLOCK = LRU_WIDTH // N_LRU_BLOCKS
CONV_WIDTH = 4
LRU_C = 8.0
D_FF = -(-8 * D_MODEL // (3 * 256)) * 256
D_IN = 3 * ATTN_WIDTH + N_IDX_HEADS * IDX_DIM + IDX_DIM + N_IDX_HEADS + 2 * LRU_WIDTH + 2 * D_MODEL
QBLOCK = 128
EPS = 1e-6

kernel_name = 'hybrid_dsa_rglru_gated_decode_step'


def _rmsnorm(x, g):
    xf = x.astype(jnp.float32)
    y = xf * lax.rsqrt(jnp.mean(xf * xf, axis=-1, keepdims=True) + EPS) * g.astype(jnp.float32)
    return y.astype(x.dtype)


def _project(x, norm1_g, w_in):
    B, T, _ = x.shape
    h = _rmsnorm(x, norm1_g)
    z = h @ w_in
    sizes = [ATTN_WIDTH, ATTN_WIDTH, ATTN_WIDTH, N_IDX_HEADS * IDX_DIM, IDX_DIM, N_IDX_HEADS,
             LRU_WIDTH, LRU_WIDTH, D_MODEL, D_MODEL]
    cuts = [int(c) for c in np.cumsum(sizes)[:-1]]
    q, k, v, qi, ki, wi, x_lru, y_lru, ga, gb = jnp.split(z, cuts, axis=-1)
    q = q.reshape(B, T, N_HEADS, HEAD_DIM)
    k = k.reshape(B, T, N_HEADS, HEAD_DIM)
    v = v.reshape(B, T, N_HEADS, HEAD_DIM)
    qi = qi.reshape(B, T, N_IDX_HEADS, IDX_DIM)
    return q, k, v, qi, ki, wi, x_lru, y_lru, ga, gb


def _index_topk(qi, wi, ki_all, qpos, n_sel):
    s = jnp.einsum('bthd,bsd->bths', qi.astype(jnp.float32), ki_all.astype(jnp.float32)) * (IDX_DIM ** -0.5)
    I = jnp.einsum('bths,bth->bts', jax.nn.relu(s), wi.astype(jnp.float32)) * (N_IDX_HEADS ** -0.5)
    kpos = jnp.arange(ki_all.shape[1])
    I = jnp.where(kpos[None, None, :] <= qpos[None, :, None], I, -jnp.inf)
    _, idx = lax.top_k(I, n_sel)
    return idx.astype(jnp.int32)


def _gather_rows(src, idx):
    return jax.vmap(lambda s, i: s[i])(src, idx)


def _sparse_attend(q, k_sel, v_sel, idx, qpos):
    B, T = q.shape[:2]
    logits = jnp.einsum('bthd,btkhd->bthk', q.astype(jnp.float32), k_sel.astype(jnp.float32)) * (HEAD_DIM ** -0.5)
    valid = idx <= qpos[None, :, None]
    logits = jnp.where(valid[:, :, None, :], logits, jnp.float32(-1e30))
    p = jax.nn.softmax(logits, axis=-1)
    o = jnp.einsum('bthk,btkhd->bthd', p, v_sel.astype(jnp.float32))
    return o.reshape(B, T, ATTN_WIDTH).astype(q.dtype)


def _prompt_attention(q, k, v, qi, ki, wi):
    B, T = q.shape[:2]
    n_sel = min(TOPK_MAX, T // 4)
    nb = T // QBLOCK

    def blocks(a):
        return jnp.swapaxes(a.reshape((B, nb, QBLOCK) + a.shape[2:]), 0, 1)

    pos = jnp.arange(T).reshape(nb, QBLOCK)

    def one_block(args):
        qb, qib, wib, posb = args
        idx = _index_topk(qib, wib, ki, posb, n_sel)
        return _sparse_attend(qb, _gather_rows(k, idx), _gather_rows(v, idx), idx, posb)

    out = lax.map(one_block, (blocks(q), blocks(qi), blocks(wi), pos))
    return jnp.swapaxes(out, 0, 1).reshape(B, T, ATTN_WIDTH)


def _sample_attention(q, k_new, v_new, qi, ki_new, wi, cache_k, cache_v, cache_kidx, page_table):
    Bd, T = q.shape[:2]
    past = page_table.shape[1] * PAGE_SIZE
    ki_past = cache_kidx[page_table].reshape(Bd, past, IDX_DIM)
    ki_all = jnp.concatenate([ki_past, ki_new.astype(ki_past.dtype)], axis=1)
    qpos = past + jnp.arange(T)
    n_sel = min(TOPK_MAX, (past + T) // 4)
    idx = _index_topk(qi, wi, ki_all, qpos, n_sel)
    from_new = idx >= past
    pidx = jnp.minimum(idx, past - 1)
    phys = jax.vmap(lambda pt, i: pt[i // PAGE_SIZE])(page_table, pidx)
    off = pidx % PAGE_SIZE
    nidx = jnp.clip(idx - past, 0, T - 1)
    k_old = cache_k[phys, off]
    v_old = cache_v[phys, off]
    k_sel = jnp.where(from_new[..., None, None], _gather_rows(k_new, nidx).astype(k_old.dtype), k_old)
    v_sel = jnp.where(from_new[..., None, None], _gather_rows(v_new, nidx).astype(v_old.dtype), v_old)
    return _sparse_attend(q, k_sel, v_sel, idx, qpos)


def _lin_combine(left, right):
    a1, b1 = left
    a2, b2 = right
    return a1 * a2, a2 * b1 + b2


def _rglru_branch(x_lru, y_lru, conv_buf, h0, conv_w, conv_b, rg_wa, rg_ba, rg_wx, rg_bx, rg_lambda):
    B, T, W = x_lru.shape
    ext = jnp.concatenate([conv_buf.astype(x_lru.dtype), x_lru], axis=1)
    xc = conv_b + sum(conv_w[j] * ext[:, j:j + T] for j in range(CONV_WIDTH))
    new_buf = ext[:, T:]
    xf = xc.astype(jnp.float32)
    xb = xf.reshape(B, T, N_LRU_BLOCKS, LRU_BLOCK)
    ra = jnp.einsum('btnc,ncd->btnd', xb, rg_wa.astype(jnp.float32)).reshape(B, T, W) + rg_ba.astype(jnp.float32)
    rx = jnp.einsum('btnc,ncd->btnd', xb, rg_wx.astype(jnp.float32)).reshape(B, T, W) + rg_bx.astype(jnp.float32)
    r = jax.nn.sigmoid(ra)
    i = jax.nn.sigmoid(rx)
    log_a = -LRU_C * r * jax.nn.softplus(-rg_lambda.astype(jnp.float32))
    a = jnp.exp(log_a)
    b = jnp.sqrt(-jnp.expm1(2.0 * log_a)) * (i * xf)
    b = b.at[:, 0].add(a[:, 0] * h0.astype(jnp.float32))
    _, h = lax.associative_scan(_lin_combine, (a, b), axis=1)
    out = h * jax.nn.gelu(y_lru.astype(jnp.float32))
    return out.astype(x_lru.dtype), new_buf, h[:, -1]


def _merge_ffn(x, attn_o, lru_o, ga, gb, w_branch_attn, w_branch_lru, w_out,
               norm2_g, w_ffn_gate, w_ffn_up, w_ffn_down, norm_f_g):
    merged = jax.nn.sigmoid(ga) * (attn_o @ w_branch_attn) + jax.nn.sigmoid(gb) * (lru_o @ w_branch_lru)
    x = x + merged @ w_out
    h = _rmsnorm(x, norm2_g)
    x = x + (jax.nn.silu(h @ w_ffn_gate) * (h @ w_ffn_up)) @ w_ffn_down
    return _rmsnorm(x, norm_f_g)


def setup_inputs(seed: int = 0) -> dict:
    key = jax.random.key(seed)
    ks = jax.random.split(key, 24)
    n_pages = PAST_LEN // PAGE_SIZE
    used = DEC_BATCH * n_pages
    n_pool = used + max(1, used // 4)
    f32 = jnp.float32
    nrm = lambda k, s, sc: jax.random.normal(k, s, f32) * sc
    page_table = jax.random.permutation(ks[0], n_pool)[:used].reshape(DEC_BATCH, n_pages).astype(jnp.int32)
    a_init = jax.random.uniform(ks[1], (LRU_WIDTH,), f32, 0.9, 0.999)
    s_init = a_init ** (1.0 / LRU_C)
    rg_lambda = jnp.log(s_init) - jnp.log1p(-s_init)
    return {
        'x_prompt': nrm(ks[2], (BATCH, SEQ, D_MODEL), 1.0),
        'x_sample': nrm(ks[3], (DEC_BATCH, DEC_SEQ, D_MODEL), 1.0),
        'cache_k': nrm(ks[4], (n_pool, PAGE_SIZE, N_HEADS, HEAD_DIM), 1.0),
        'cache_v': nrm(ks[5], (n_pool, PAGE_SIZE, N_HEADS, HEAD_DIM), 1.0),
        'cache_kidx': nrm(ks[6], (n_pool, PAGE_SIZE, IDX_DIM), 1.0),
        'state_conv': nrm(ks[7], (DEC_BATCH, CONV_WIDTH - 1, LRU_WIDTH), 1.0),
        'state_rglru': nrm(ks[8], (DEC_BATCH, LRU_WIDTH), 0.5),
        'page_table': page_table,
        'norm1_g': 1.0 + nrm(ks[9], (D_MODEL,), 0.01),
        'w_in': nrm(ks[10], (D_MODEL, D_IN), D_MODEL ** -0.5),
        'conv_w': nrm(ks[11], (CONV_WIDTH, LRU_WIDTH), CONV_WIDTH ** -0.5),
        'conv_b': nrm(ks[12], (LRU_WIDTH,), 0.01),
        'rg_wa': nrm(ks[13], (N_LRU_BLOCKS, LRU_BLOCK, LRU_BLOCK), LRU_BLOCK ** -0.5),
        'rg_ba': nrm(ks[14], (LRU_WIDTH,), 0.01),
        'rg_wx': nrm(ks[15], (N_LRU_BLOCKS, LRU_BLOCK, LRU_BLOCK), LRU_BLOCK ** -0.5),
        'rg_bx': nrm(ks[16], (LRU_WIDTH,), 0.01),
        'rg_lambda': rg_lambda,
        'w_branch_attn': nrm(ks[17], (ATTN_WIDTH, D_MODEL), ATTN_WIDTH ** -0.5),
        'w_branch_lru': nrm(ks[18], (LRU_WIDTH, D_MODEL), LRU_WIDTH ** -0.5),
        'w_out': nrm(ks[19], (D_MODEL, D_MODEL), D_MODEL ** -0.5),
        'norm2_g': 1.0 + nrm(ks[20], (D_MODEL,), 0.01),
        'w_ffn_gate': nrm(ks[21], (D_MODEL, D_FF), D_MODEL ** -0.5),
        'w_ffn_up': nrm(ks[22], (D_MODEL, D_FF), D_MODEL ** -0.5),
        'w_ffn_down': nrm(ks[23], (D_FF, D_MODEL), D_FF ** -0.5),
        'norm_f_g': 1.0 + nrm(jax.random.fold_in(key, 99), (D_MODEL,), 0.01),
    }


def reference(x_prompt, x_sample, cache_k, cache_v, cache_kidx, state_conv, state_rglru, page_table,
              norm1_g, w_in, conv_w, conv_b, rg_wa, rg_ba, rg_wx, rg_bx, rg_lambda,
              w_branch_attn, w_branch_lru, w_out, norm2_g, w_ffn_gate, w_ffn_up, w_ffn_down, norm_f_g):
    lru_w = (conv_w, conv_b, rg_wa, rg_ba, rg_wx, rg_bx, rg_lambda)
    tail_w = (w_branch_attn, w_branch_lru, w_out, norm2_g, w_ffn_gate, w_ffn_up, w_ffn_down, norm_f_g)

    B = x_prompt.shape[0]
    q, k_p, v_p, qi, ki_p, wi, x_lru, y_lru, ga, gb = _project(x_prompt, norm1_g, w_in)
    attn_p = _prompt_attention(q, k_p, v_p, qi, ki_p, wi)
    buf0 = jnp.zeros((B, CONV_WIDTH - 1, LRU_WIDTH), x_prompt.dtype)
    h00 = jnp.zeros((B, LRU_WIDTH), jnp.float32)
    lru_p, conv_p, h_p = _rglru_branch(x_lru, y_lru, buf0, h00, *lru_w)
    y_prompt = _merge_ffn(x_prompt, attn_p, lru_p, ga, gb, *tail_w)

    q, k_s, v_s, qi, ki_s, wi, x_lru, y_lru, ga, gb = _project(x_sample, norm1_g, w_in)
    attn_s = _sample_attention(q, k_s, v_s, qi, ki_s, wi, cache_k, cache_v, cache_kidx, page_table)
    lru_s, conv_s, h_s = _rglru_branch(x_lru, y_lru, state_conv, state_rglru, *lru_w)
    y_sample = _merge_ffn(x_sample, attn_s, lru_s, ga, gb, *tail_w)

    return (y_prompt, y_sample,
            k_p, v_p, ki_p, conv_p.astype(state_conv.dtype), h_p.astype(state_rglru.dtype),
            k_s, v_s, ki_s, conv_s.astype(state_conv.dtype), h_s.astype(state_rglru.dtype))
```

```python
import functools

import jax
import jax.numpy as jnp
import numpy as np
from jax import lax
from jax.experimental import pallas as pl
from jax.experimental.pallas import tpu as pltpu

TOPK_MAX = 256
LRU_C = 8.0
EPS = 1e-6
MASK_VALUE = -1e30

V7X_VMEM_BYTES = 64 * 1024 * 1024
VMEM_LIMIT_BYTES = 56 * 1024 * 1024
LANES = 128
SUBLANES = 8

BF16 = jnp.bfloat16
F32 = jnp.float32
I32 = jnp.int32

_NT_DIMS = (((1,), (1,)), ((), ()))


def _params(*sem):
    return pltpu.CompilerParams(dimension_semantics=sem, vmem_limit_bytes=VMEM_LIMIT_BYTES)


def _resident(block_shape, index_map):
    return pl.BlockSpec(block_shape, index_map, pipeline_mode=pl.Buffered(1))


def _row_block(m, target):
    tm = min(m, target)
    assert m % tm == 0, (m, tm)
    return tm


def _rmsnorm_kernel(x_ref, g_ref, o_ref):
    x = x_ref[...]
    y = x * lax.rsqrt(jnp.mean(x * x, axis=-1, keepdims=True) + EPS) * g_ref[...]
    o_ref[...] = y.astype(o_ref.dtype)


def _rmsnorm_bf16(x, g):
    m, d = x.shape
    tm = _row_block(m, 512)
    return pl.pallas_call(
        _rmsnorm_kernel,
        grid=(m // tm,),
        in_specs=[pl.BlockSpec((tm, d), lambda i: (i, 0)), pl.BlockSpec((1, d), lambda i: (0, 0))],
        out_specs=pl.BlockSpec((tm, d), lambda i: (i, 0)),
        out_shape=jax.ShapeDtypeStruct((m, d), BF16),
        compiler_params=_params("parallel"),
        name="rmsnorm_bf16",
    )(x, g.reshape(1, d))


def _matmul_kernel(a_ref, w_ref, *o_refs):
    acc = jnp.dot(a_ref[...], w_ref[...], preferred_element_type=F32)
    for o_ref in o_refs:
        o_ref[...] = acc.astype(o_ref.dtype)


def _matmul(a, w, out_dtypes):
    m, k = a.shape
    n = w.shape[1]
    tm = _row_block(m, 512)
    tn = _row_block(n, 1024)
    outs = pl.pallas_call(
        _matmul_kernel,
        grid=(n // tn, m // tm),
        in_specs=[pl.BlockSpec((tm, k), lambda j, i: (i, 0)), pl.BlockSpec((k, tn), lambda j, i: (0, j))],
        out_specs=[pl.BlockSpec((tm, tn), lambda j, i: (i, j)) for _ in out_dtypes],
        out_shape=[jax.ShapeDtypeStruct((m, n), dt) for dt in out_dtypes],
        compiler_params=_params("parallel", "parallel"),
        name="proj_matmul",
    )(a, w)
    return outs


def _sortable_key(x):
    x = jnp.where(x == 0.0, 0.0, x)
    bits = pltpu.bitcast(x, I32)
    return bits ^ ((bits >> 31) & jnp.int32(0x7FFFFFFF))


def _kth_largest_key(count_ge, k, shape):
    int_min = jnp.int32(-(2**31))
    zero = jnp.zeros(shape, I32)
    ans = jnp.where(count_ge(zero) >= k, zero, jnp.full(shape, int_min, I32))

    def body(it, ans):
        cand = ans | (jnp.int32(1) << (jnp.int32(30) - it))
        return jnp.where(count_ge(cand) >= k, cand, ans)

    return lax.fori_loop(0, 31, body, ans)


IDX_SUB = 256
KV_CHUNK = 512


def _prompt_attn_kernel(q_ref, qi_ref, kiwi_ref, ki_ref, k_ref, v_ref, o_ref,
                        wb_scr, keys_scr, m_scr, l_scr, acc_scr,
                        *, tq, n_heads, head_dim, n_idx_heads, idx_dim, n_sel):
    i = pl.program_id(0)
    row0 = i * tq
    n_chunks = (row0 + tq + KV_CHUNK - 1) // KV_CHUNK
    idx_scale = (idx_dim ** -0.5) * (n_idx_heads ** -0.5)
    rows = row0 + lax.broadcasted_iota(I32, (tq, 1), 0)

    wi = kiwi_ref[:, idx_dim:idx_dim + n_idx_heads] * idx_scale
    for h in range(n_idx_heads):
        wb_scr[h] = jnp.broadcast_to(wi[:, h:h + 1], (tq, LANES))

    def score_chunk(c, carry):
        for sub in range(KV_CHUNK // IDX_SUB):
            col0 = pl.multiple_of(c * KV_CHUNK + sub * IDX_SUB, IDX_SUB)
            ki_c = ki_ref[pl.ds(col0, IDX_SUB), :]
            acc = jnp.zeros((tq, IDX_SUB), F32)
            for h in range(n_idx_heads):
                s = lax.dot_general(qi_ref[:, h * idx_dim:(h + 1) * idx_dim], ki_c, _NT_DIMS,
                                    preferred_element_type=F32)
                wb = wb_scr[h]
                acc = acc + jnp.maximum(s, 0.0) * jnp.concatenate([wb] * (IDX_SUB // LANES), axis=1)
            cols = col0 + lax.broadcasted_iota(I32, (1, IDX_SUB), 1)
            acc = jnp.where(cols <= rows, acc, -jnp.inf)
            keys_scr[:, pl.ds(col0, IDX_SUB)] = _sortable_key(acc)
        return carry

    lax.fori_loop(0, n_chunks, score_chunk, 0)

    def count_ge(cand):
        cand_b = jnp.broadcast_to(cand, (tq, LANES))

        def body(c, acc):
            col0 = pl.multiple_of(c * KV_CHUNK, KV_CHUNK)
            for j in range(KV_CHUNK // LANES):
                blk = keys_scr[:, pl.ds(col0 + j * LANES, LANES)]
                acc = acc + jnp.where(blk >= cand_b, 1, 0)
            return acc

        acc = lax.fori_loop(0, n_chunks, body, jnp.zeros((tq, LANES), I32))
        return jnp.sum(acc, axis=1, keepdims=True)

    thr = _kth_largest_key(count_ge, n_sel, (tq, 1))

    m_scr[...] = jnp.full(m_scr.shape, MASK_VALUE, F32)
    l_scr[...] = jnp.zeros(l_scr.shape, F32)
    acc_scr[...] = jnp.zeros(acc_scr.shape, F32)
    scale = head_dim ** -0.5

    def attend_chunk(c, carry):
        col0 = pl.multiple_of(c * KV_CHUNK, KV_CHUNK)
        cols = col0 + lax.broadcasted_iota(I32, (1, KV_CHUNK), 1)
        sel = (keys_scr[:, pl.ds(col0, KV_CHUNK)] >= thr) & (cols <= rows)
        for h in range(n_heads):
            hs = slice(h * head_dim, (h + 1) * head_dim)
            s = lax.dot_general(q_ref[:, hs], k_ref[pl.ds(col0, KV_CHUNK), hs], _NT_DIMS,
                                preferred_element_type=F32) * scale
            s = jnp.where(sel, s, MASK_VALUE)
            m_prev = m_scr[h][:, :1]
            m_new = jnp.maximum(m_prev, jnp.max(s, axis=1, keepdims=True))
            alpha = jnp.exp(m_prev - m_new)
            p = jnp.exp(s - m_new)
            l_scr[h] = jnp.broadcast_to(alpha * l_scr[h][:, :1] + jnp.sum(p, axis=1, keepdims=True),
                                        (tq, LANES))
            acc_scr[h] = alpha * acc_scr[h] + jnp.dot(p.astype(BF16), v_ref[pl.ds(col0, KV_CHUNK), hs],
                                                      preferred_element_type=F32)
            m_scr[h] = jnp.broadcast_to(m_new, (tq, LANES))
        return carry

    lax.fori_loop(0, n_chunks, attend_chunk, 0)

    for h in range(n_heads):
        o_ref[:, h * head_dim:(h + 1) * head_dim] = (acc_scr[h] / l_scr[h][:, :1]).astype(o_ref.dtype)


def _prompt_attention(q, qi, kiwi, ki, k, v, *, n_heads, head_dim, n_idx_heads, idx_dim):
    t = q.shape[0]
    tq = _row_block(t, 256)
    assert t % KV_CHUNK == 0 and t >= TOPK_MAX * 2
    n_sel = min(TOPK_MAX, t // 4)
    aw = n_heads * head_dim
    kern = functools.partial(_prompt_attn_kernel, tq=tq, n_heads=n_heads, head_dim=head_dim,
                             n_idx_heads=n_idx_heads, idx_dim=idx_dim, n_sel=n_sel)
    return pl.pallas_call(
        kern,
        grid=(t // tq,),
        in_specs=[
            pl.BlockSpec((tq, aw), lambda i: (i, 0)),
            pl.BlockSpec((tq, n_idx_heads * idx_dim), lambda i: (i, 0)),
            pl.BlockSpec((tq, LANES), lambda i: (i, 0)),
            _resident((t, idx_dim), lambda i: (0, 0)),
            _resident((t, aw), lambda i: (0, 0)),
            _resident((t, aw), lambda i: (0, 0)),
        ],
        out_specs=pl.BlockSpec((tq, aw), lambda i: (i, 0)),
        out_shape=jax.ShapeDtypeStruct((t, aw), BF16),
        scratch_shapes=[
            pltpu.VMEM((n_idx_heads, tq, LANES), F32),
            pltpu.VMEM((tq, t), I32),
            pltpu.VMEM((n_heads, tq, LANES), F32),
            pltpu.VMEM((n_heads, tq, LANES), F32),
            pltpu.VMEM((n_heads, tq, head_dim), F32),
        ],
        compiler_params=_params("arbitrary"),
        name="prompt_sparse_attention",
    )(q, qi, kiwi, ki, k, v)


def _softplus(x):
    return jnp.maximum(x, 0.0) + jnp.log1p(jnp.exp(-jnp.abs(x)))


def _split_bf16(x):
    hi = x.astype(BF16)
    lo = (x - hi.astype(F32)).astype(BF16)
    return hi, lo


def _lru_gates(xc, wa_ref, ba_ref, wx_ref, bx_ref, lam_ref):
    n_blocks, blk, _ = wa_ref.shape
    ra, rx = [], []
    for n in range(n_blocks):
        x_hi, x_lo = _split_bf16(xc[:, n * blk:(n + 1) * blk])
        for w_ref, dst in ((wa_ref, ra), (wx_ref, rx)):
            w_hi, w_lo = _split_bf16(w_ref[n])
            dst.append(jnp.dot(x_hi, w_hi, preferred_element_type=F32)
                       + jnp.dot(x_hi, w_lo, preferred_element_type=F32)
                       + jnp.dot(x_lo, w_hi, preferred_element_type=F32))
    r = jax.nn.sigmoid(jnp.concatenate(ra, axis=1) + ba_ref[...])
    gate_i = jax.nn.sigmoid(jnp.concatenate(rx, axis=1) + bx_ref[...])
    log_a = (-LRU_C) * r * _softplus(-lam_ref[...])
    a = jnp.exp(log_a)
    b = jnp.sqrt(jnp.tanh(-log_a) * (1.0 + a * a)) * (gate_i * xc)
    return a, b


def _lru_prompt_kernel(x_ref, y_ref, cw_ref, cb_ref, wa_ref, ba_ref, wx_ref, bx_ref, lam_ref,
                       o_ref, hlast_ref, ext_scr, a_scr, b_scr, h_scr, *, tb, conv_width):
    t = pl.program_id(0)
    pad = SUBLANES
    assert conv_width - 1 <= pad

    @pl.when(t == 0)
    def _():
        ext_scr[0:pad, :] = jnp.zeros((pad, ext_scr.shape[1]), F32)
        hlast_ref[...] = jnp.zeros(hlast_ref.shape, F32)

    ext_scr[pad:pad + tb, :] = x_ref[...]
    conv = None
    for j in range(conv_width):
        term = cw_ref[j:j + 1, :] * ext_scr[pl.ds(pad - (conv_width - 1) + j, tb), :]
        conv = term if conv is None else conv + term
    xc = cb_ref[...] + conv
    a, b = _lru_gates(xc, wa_ref, ba_ref, wx_ref, bx_ref, lam_ref)
    a_scr[...] = a
    b_scr[...] = b

    sub = lax.broadcasted_iota(I32, (SUBLANES, 1), 0)

    def tile(j, h_prev):
        r0 = pl.multiple_of(j * SUBLANES, SUBLANES)
        at = a_scr[pl.ds(r0, SUBLANES), :]
        bt = b_scr[pl.ds(r0, SUBLANES), :]
        for d in (1, 2, 4):
            a_sh = pltpu.roll(at, d, 0)
            b_sh = pltpu.roll(bt, d, 0)
            ok = sub >= d
            bt = jnp.where(ok, at * b_sh + bt, bt)
            at = jnp.where(ok, at * a_sh, at)
        h = at * h_prev + bt
        h_scr[pl.ds(r0, SUBLANES), :] = h
        return jnp.broadcast_to(h[SUBLANES - 1:SUBLANES, :], h.shape)

    h_last = lax.fori_loop(0, tb // SUBLANES, tile, hlast_ref[...])
    hlast_ref[...] = h_last
    ext_scr[0:pad, :] = ext_scr[tb:tb + pad, :]
    o_ref[...] = (h_scr[...] * jax.nn.gelu(y_ref[...])).astype(o_ref.dtype)


def _lru_prompt(x_lru, y_lru, conv_w, conv_b, rg_wa, rg_ba, rg_wx, rg_bx, rg_lambda):
    t, w = x_lru.shape
    tb = _row_block(t, 256)
    cw = conv_w.shape[0]
    vec = lambda a: a.reshape(1, w)
    full = lambda shape: pl.BlockSpec(shape, lambda i: (0,) * len(shape))
    out, h_last = pl.pallas_call(
        functools.partial(_lru_prompt_kernel, tb=tb, conv_width=cw),
        grid=(t // tb,),
        in_specs=[pl.BlockSpec((tb, w), lambda i: (i, 0)), pl.BlockSpec((tb, w), lambda i: (i, 0)),
                  full((cw, w)), full((1, w)), full(rg_wa.shape), full((1, w)), full(rg_wx.shape),
                  full((1, w)), full((1, w))],
        out_specs=[pl.BlockSpec((tb, w), lambda i: (i, 0)), full((SUBLANES, w))],
        out_shape=[jax.ShapeDtypeStruct((t, w), BF16), jax.ShapeDtypeStruct((SUBLANES, w), F32)],
        scratch_shapes=[pltpu.VMEM((tb + SUBLANES, w), F32), pltpu.VMEM((tb, w), F32),
                        pltpu.VMEM((tb, w), F32), pltpu.VMEM((tb, w), F32)],
        compiler_params=_params("arbitrary"),
        name="rglru_prompt",
    )(x_lru, y_lru, conv_w, vec(conv_b), rg_wa, vec(rg_ba), rg_wx, vec(rg_bx), vec(rg_lambda))
    return out, h_last[0]


def _lru_step_kernel(x_ref, y_ref, sc_ref, h0_ref, cw_ref, cb_ref, wa_ref, ba_ref, wx_ref, bx_ref,
                     lam_ref, o_ref, h_ref, *, conv_width):
    conv = None
    for j in range(conv_width):
        src = x_ref[...] if j == conv_width - 1 else sc_ref[j]
        term = cw_ref[j:j + 1, :] * src
        conv = term if conv is None else conv + term
    xc = cb_ref[...] + conv
    a, b = _lru_gates(xc, wa_ref, ba_ref, wx_ref, bx_ref, lam_ref)
    h = b + a * h0_ref[...]
    h_ref[...] = h
    o_ref[...] = (h * jax.nn.gelu(y_ref[...])).astype(o_ref.dtype)


def _lru_step(x_lru, y_lru, state_conv_t, h0, conv_w, conv_b, rg_wa, rg_ba, rg_wx, rg_bx, rg_lambda):
    b, w = x_lru.shape
    vec = lambda a: a.reshape(1, w)
    return pl.pallas_call(
        functools.partial(_lru_step_kernel, conv_width=conv_w.shape[0]),
        out_shape=[jax.ShapeDtypeStruct((b, w), BF16), jax.ShapeDtypeStruct((b, w), F32)],
        compiler_params=pltpu.CompilerParams(vmem_limit_bytes=VMEM_LIMIT_BYTES),
        name="rglru_step",
    )(x_lru, y_lru, state_conv_t, h0, conv_w, vec(conv_b), rg_wa, vec(rg_ba), rg_wx, vec(rg_bx),
      vec(rg_lambda))


def _merge_kernel(x_ref, ao_ref, lo_ref, ga_ref, gb_ref, wba_ref, wbl_ref, wout_ref, g2_ref,
                  x1_ref, h2_ref):
    attn = jnp.dot(ao_ref[...], wba_ref[...], preferred_element_type=F32)
    lru = jnp.dot(lo_ref[...], wbl_ref[...], preferred_element_type=F32)
    merged = jax.nn.sigmoid(ga_ref[...]) * attn + jax.nn.sigmoid(gb_ref[...]) * lru
    x1 = x_ref[...] + jnp.dot(merged.astype(BF16), wout_ref[...], preferred_element_type=F32)
    x1_ref[...] = x1
    h2 = x1 * lax.rsqrt(jnp.mean(x1 * x1, axis=-1, keepdims=True) + EPS) * g2_ref[...]
    h2_ref[...] = h2.astype(h2_ref.dtype)


def _merge(x, attn_o, lru_o, ga, gb, w_ba, w_bl, w_out, norm2_g):
    m, d = x.shape
    tm = _row_block(m, 256)
    row = lambda width: pl.BlockSpec((tm, width), lambda i: (i, 0))
    return pl.pallas_call(
        _merge_kernel,
        grid=(m // tm,),
        in_specs=[row(d), row(attn_o.shape[1]), row(lru_o.shape[1]), row(d), row(d),
                  _resident(w_ba.shape, lambda i: (0, 0)), _resident(w_bl.shape, lambda i: (0, 0)),
                  _resident(w_out.shape, lambda i: (0, 0)), pl.BlockSpec((1, d), lambda i: (0, 0))],
        out_specs=[row(d), row(d)],
        out_shape=[jax.ShapeDtypeStruct((m, d), F32), jax.ShapeDtypeStruct((m, d), BF16)],
        compiler_params=_params("parallel"),
        name="gated_merge",
    )(x, attn_o, lru_o, ga, gb, w_ba, w_bl, w_out, norm2_g.reshape(1, d))


def _ffn_kernel(h_ref, x1_ref, wg_ref, wu_ref, wd_ref, gf_ref, y_ref, acc_scr):
    f = pl.program_id(1)

    @pl.when(f == 0)
    def _():
        acc_scr[...] = jnp.zeros(acc_scr.shape, F32)

    h = h_ref[...]
    gate = jnp.dot(h, wg_ref[...], preferred_element_type=F32)
    up = jnp.dot(h, wu_ref[...], preferred_element_type=F32)
    act = (jax.nn.silu(gate) * up).astype(BF16)
    acc_scr[...] += jnp.dot(act, wd_ref[...], preferred_element_type=F32)

    @pl.when(f == pl.num_programs(1) - 1)
    def _():
        x2 = x1_ref[...] + acc_scr[...]
        y_ref[...] = x2 * lax.rsqrt(jnp.mean(x2 * x2, axis=-1, keepdims=True) + EPS) * gf_ref[...]


def _ffn(h2, x1, w_gate, w_up, w_down, norm_f_g):
    m, d = x1.shape
    ff = w_gate.shape[1]
    tm = _row_block(m, 512)
    tf = 512
    assert ff % tf == 0
    return pl.pallas_call(
        _ffn_kernel,
        grid=(m // tm, ff // tf),
        in_specs=[pl.BlockSpec((tm, d), lambda i, f: (i, 0)), pl.BlockSpec((tm, d), lambda i, f: (i, 0)),
                  pl.BlockSpec((d, tf), lambda i, f: (0, f)), pl.BlockSpec((d, tf), lambda i, f: (0, f)),
                  pl.BlockSpec((tf, d), lambda i, f: (f, 0)), pl.BlockSpec((1, d), lambda i, f: (0, 0))],
        out_specs=pl.BlockSpec((tm, d), lambda i, f: (i, 0)),
        out_shape=jax.ShapeDtypeStruct((m, d), F32),
        scratch_shapes=[pltpu.VMEM((tm, d), F32)],
        compiler_params=_params("parallel", "arbitrary"),
        name="swiglu_ffn",
    )(h2, x1, w_gate, w_up, w_down, norm_f_g.reshape(1, d))


SCORE_PAGES = 16


def _sample_select_kernel(pt_ref, qi_ref, wi_ref, kin_ref, kidx_hbm, idx_ref,
                          kibuf, sem, score_scr, slot_scr,
                          *, n_pages, page, n_idx_heads, idx_dim, n_sel):
    b = pl.program_id(0)
    past = n_pages * page
    idx_scale = (idx_dim ** -0.5) * (n_idx_heads ** -0.5)

    def page_copy(p):
        return pltpu.make_async_copy(kidx_hbm.at[pt_ref[b, p]], kibuf.at[p], sem)

    def start(p, c):
        page_copy(p).start()
        return c

    lax.fori_loop(0, n_pages, start, 0)

    def wait(p, c):
        page_copy(p).wait()
        return c

    lax.fori_loop(0, n_pages, wait, 0)

    qi = qi_ref[0]
    wcol = wi_ref[0] * idx_scale
    wb = jnp.broadcast_to(wcol, (n_idx_heads, page))

    def score(g, c):
        p0 = pl.multiple_of(g * SCORE_PAGES, SCORE_PAGES)
        kblk = kibuf[pl.ds(p0, SCORE_PAGES)].reshape(SCORE_PAGES * page, idx_dim).astype(BF16)
        s = lax.dot_general(qi, kblk, _NT_DIMS, preferred_element_type=F32)
        rows = []
        for j in range(SCORE_PAGES):
            sj = jnp.maximum(s[:, j * page:(j + 1) * page], 0.0) * wb
            rows.append(jnp.sum(sj, axis=0, keepdims=True))
        score_scr[pl.ds(p0, SCORE_PAGES), :] = _sortable_key(jnp.concatenate(rows, axis=0))
        return c

    lax.fori_loop(0, n_pages // SCORE_PAGES, score, 0)

    s_new = jnp.sum(qi.astype(F32) * kin_ref[0].astype(BF16).astype(F32), axis=1, keepdims=True)
    i_new = jnp.sum(jnp.maximum(s_new, 0.0) * wcol, axis=0, keepdims=True)
    key_new = _sortable_key(i_new)

    def count_ge(cand):
        keys = score_scr[...]
        cnt = jnp.sum(jnp.where(keys >= cand, 1, 0), axis=1, keepdims=True)
        cnt = jnp.sum(cnt, axis=0, keepdims=True)
        return cnt + jnp.where(key_new >= cand, 1, 0)

    thr = _kth_largest_key(count_ge, n_sel, (1, 1))
    sel = score_scr[...] >= thr
    sel_new = key_new >= thr

    self_bf = jnp.where(sel, 1.0, 0.0).astype(BF16)
    r_i = lax.broadcasted_iota(I32, (page, page), 0)
    c_i = lax.broadcasted_iota(I32, (page, page), 1)
    incl = jnp.dot(self_bf, jnp.where(r_i <= c_i, 1.0, 0.0).astype(BF16), preferred_element_type=F32)
    row_cnt = jnp.broadcast_to(incl[:, page - 1:page], (n_pages, page)).astype(BF16)
    rp = lax.broadcasted_iota(I32, (n_pages, n_pages), 0)
    cp = lax.broadcasted_iota(I32, (n_pages, n_pages), 1)
    base = jnp.dot(jnp.where(cp < rp, 1.0, 0.0).astype(BF16), row_cnt, preferred_element_type=F32)
    slot = (base + incl).astype(I32) - 1
    slot_scr[...] = jnp.where(sel, slot, -1)

    j_iota = lax.broadcasted_iota(I32, (n_sel, 1), 0)
    off_iota = lax.broadcasted_iota(I32, (1, page), 1)

    def gather(p, acc):
        srow = slot_scr[pl.ds(p, 1), :]
        return acc + jnp.where(srow == j_iota, p * page + off_iota, 0)

    acc = lax.fori_loop(0, n_pages, gather, jnp.zeros((n_sel, page), I32))
    idx = jnp.sum(acc, axis=1, keepdims=True)
    idx = jnp.where(sel_new & (j_iota == n_sel - 1), past, idx)
    idx_ref[0] = jnp.broadcast_to(idx, (n_sel, LANES))


def _sample_select(qi, wi, ki_new, cache_kidx, page_table, *, n_sel):
    bsz, n_idx_heads, idx_dim = qi.shape
    n_pages = page_table.shape[1]
    page = cache_kidx.shape[1]
    assert page == LANES and n_pages % SCORE_PAGES == 0
    kern = functools.partial(_sample_select_kernel, n_pages=n_pages, page=page, n_idx_heads=n_idx_heads,
                             idx_dim=idx_dim, n_sel=n_sel)
    idx = pl.pallas_call(
        kern,
        grid_spec=pltpu.PrefetchScalarGridSpec(
            num_scalar_prefetch=1,
            grid=(bsz,),
            in_specs=[pl.BlockSpec((1, n_idx_heads, idx_dim), lambda b, pt: (b, 0, 0)),
                      pl.BlockSpec((1, n_idx_heads, 1), lambda b, pt: (b, 0, 0)),
                      pl.BlockSpec((1, 1, idx_dim), lambda b, pt: (b, 0, 0)),
                      pl.BlockSpec(memory_space=pl.ANY)],
            out_specs=pl.BlockSpec((1, n_sel, LANES), lambda b, pt: (b, 0, 0)),
            scratch_shapes=[pltpu.VMEM((n_pages, page, idx_dim), F32), pltpu.SemaphoreType.DMA(()),
                            pltpu.VMEM((n_pages, page), I32), pltpu.VMEM((n_pages, page), I32)],
        ),
        out_shape=jax.ShapeDtypeStruct((bsz, n_sel, LANES), I32),
        compiler_params=_params("arbitrary"),
        name="sample_index_select",
    )(page_table, qi, wi, ki_new, cache_kidx)
    return idx[:, :, 0]


def _sample_attend_kernel(idx_sm, pt_sm, idxv_ref, q_ref, kn_ref, vn_ref, ck_hbm, cv_hbm, o_ref,
                          kbuf, vbuf, sems, *, n_sel, page, past, n_heads, head_dim):
    b = pl.program_id(0)

    def copies(j):
        pidx = jnp.minimum(idx_sm[b, j], past - 1)
        phys = pt_sm[b, lax.shift_right_logical(pidx, int(np.log2(page)))]
        off = pidx & (page - 1)
        return (pltpu.make_async_copy(ck_hbm.at[phys, off], kbuf.at[j], sems.at[0]),
                pltpu.make_async_copy(cv_hbm.at[phys, off], vbuf.at[j], sems.at[1]))

    def start(j, c):
        ck, cv = copies(j)
        ck.start()
        cv.start()
        return c

    lax.fori_loop(0, n_sel, start, 0)

    def wait(j, c):
        ck, cv = copies(j)
        ck.wait()
        cv.wait()
        return c

    lax.fori_loop(0, n_sel, wait, 0)

    from_new = idxv_ref[0] >= past
    scale = head_dim ** -0.5
    for h in range(n_heads):
        k_h = jnp.where(from_new, kn_ref[0, h:h + 1, :], kbuf[:, h, :])
        v_h = jnp.where(from_new, vn_ref[0, h:h + 1, :], vbuf[:, h, :])
        q_h = q_ref[0, h:h + 1, :]
        s = lax.dot_general(q_h, k_h.astype(BF16), _NT_DIMS, preferred_element_type=F32) * scale
        m = jnp.max(s, axis=1, keepdims=True)
        p = jnp.exp(s - m)
        o = jnp.dot(p.astype(BF16), v_h.astype(BF16), preferred_element_type=F32)
        o_ref[0, h:h + 1, :] = (o / jnp.sum(p, axis=1, keepdims=True)).astype(o_ref.dtype)


def _sample_attend(idx, page_table, q, k_new, v_new, cache_k, cache_v):
    bsz, n_sel = idx.shape
    _, page, n_heads, head_dim = cache_k.shape
    past = page_table.shape[1] * page
    assert page & (page - 1) == 0
    kern = functools.partial(_sample_attend_kernel, n_sel=n_sel, page=page, past=past,
                             n_heads=n_heads, head_dim=head_dim)
    per_seq = lambda: pl.BlockSpec((1, n_heads, head_dim), lambda b, i_sm, p_sm: (b, 0, 0))
    return pl.pallas_call(
        kern,
        grid_spec=pltpu.PrefetchScalarGridSpec(
            num_scalar_prefetch=2,
            grid=(bsz,),
            in_specs=[pl.BlockSpec((1, n_sel, 1), lambda b, i_sm, p_sm: (b, 0, 0)),
                      per_seq(), per_seq(), per_seq(),
                      pl.BlockSpec(memory_space=pl.ANY), pl.BlockSpec(memory_space=pl.ANY)],
            out_specs=per_seq(),
            scratch_shapes=[pltpu.VMEM((n_sel, n_heads, head_dim), F32),
                            pltpu.VMEM((n_sel, n_heads, head_dim), F32),
                            pltpu.SemaphoreType.DMA((2,))],
        ),
        out_shape=jax.ShapeDtypeStruct((bsz, n_heads, head_dim), BF16),
        compiler_params=_params("arbitrary"),
        name="sample_sparse_attention",
    )(idx, page_table, idx[:, :, None], q, k_new, v_new, cache_k, cache_v)


def kernel(x_prompt, x_sample, cache_k, cache_v, cache_kidx, state_conv, state_rglru, page_table,
           norm1_g, w_in, conv_w, conv_b, rg_wa, rg_ba, rg_wx, rg_bx, rg_lambda,
           w_branch_attn, w_branch_lru, w_out, norm2_g, w_ffn_gate, w_ffn_up, w_ffn_down, norm_f_g):
    bp, seq, d_model = x_prompt.shape
    bd, dec_seq, _ = x_sample.shape
    _, page, n_heads, head_dim = cache_k.shape
    idx_dim = cache_kidx.shape[2]
    lru_w = conv_w.shape[1]
    conv_width = conv_w.shape[0]
    attn_w = n_heads * head_dim
    n_idx_heads = (w_in.shape[1] - 3 * attn_w - idx_dim - 2 * lru_w - 2 * d_model) // (idx_dim + 1)
    assert bp == 1 and dec_seq == 1
    assert idx_dim + n_idx_heads <= LANES

    sizes = [attn_w, attn_w, attn_w, n_idx_heads * idx_dim, idx_dim + n_idx_heads, lru_w, lru_w,
             d_model, d_model]
    cuts = np.cumsum([0] + sizes)
    w_q, w_k, w_v, w_qi, w_kiwi, w_xl, w_yl, w_ga, w_gb = (
        w_in[:, cuts[n]:cuts[n + 1]].astype(BF16) for n in range(len(sizes)))
    w_kiwi = jnp.pad(w_kiwi, ((0, 0), (0, LANES - w_kiwi.shape[1])))
    tail_w = (w_branch_attn.astype(BF16), w_branch_lru.astype(BF16), w_out.astype(BF16), norm2_g)
    ffn_w = (w_ffn_gate.astype(BF16), w_ffn_up.astype(BF16), w_ffn_down.astype(BF16), norm_f_g)
    lru_p = (conv_w, conv_b, rg_wa, rg_ba, rg_wx, rg_bx, rg_lambda)

    def project(x2d):
        h = _rmsnorm_bf16(x2d, norm1_g)
        (q,) = _matmul(h, w_q, [BF16])
        k, k_bf = _matmul(h, w_k, [F32, BF16])
        v, v_bf = _matmul(h, w_v, [F32, BF16])
        (qi,) = _matmul(h, w_qi, [BF16])
        (kiwi,) = _matmul(h, w_kiwi, [F32])
        (x_lru,) = _matmul(h, w_xl, [F32])
        (y_lru,) = _matmul(h, w_yl, [F32])
        (ga,) = _matmul(h, w_ga, [F32])
        (gb,) = _matmul(h, w_gb, [F32])
        return q, k, k_bf, v, v_bf, qi, kiwi, x_lru, y_lru, ga, gb

    def finish(x2d, attn_o, lru_o, ga, gb):
        x1, h2 = _merge(x2d, attn_o, lru_o, ga, gb, *tail_w)
        return _ffn(h2, x1, *ffn_w)

    xp = x_prompt.reshape(seq, d_model)
    q, k_p, k_bf, v_p, v_bf, qi, kiwi, x_lru, y_lru, ga, gb = project(xp)
    ki_p = kiwi[:, :idx_dim]
    attn_p = _prompt_attention(q, qi, kiwi, ki_p.astype(BF16), k_bf, v_bf, n_heads=n_heads,
                               head_dim=head_dim, n_idx_heads=n_idx_heads, idx_dim=idx_dim)
    lru_o, h_p = _lru_prompt(x_lru, y_lru, *lru_p)
    y_prompt = finish(xp, attn_p, lru_o, ga, gb).reshape(bp, seq, d_model)
    conv_p = x_lru[seq - (conv_width - 1):].reshape(bp, conv_width - 1, lru_w)

    xs = x_sample.reshape(bd, d_model)
    q, k_s, _, v_s, _, qi, kiwi, x_lru, y_lru, ga, gb = project(xs)
    ki_s = kiwi[:, :idx_dim]
    wi_s = kiwi[:, idx_dim:idx_dim + n_idx_heads]
    past = page_table.shape[1] * page
    n_sel = min(TOPK_MAX, (past + dec_seq) // 4)
    idx = _sample_select(qi.reshape(bd, n_idx_heads, idx_dim), wi_s[:, :, None], ki_s[:, None, :],
                         cache_kidx, page_table, n_sel=n_sel)
    hd = (bd, n_heads, head_dim)
    attn_s = _sample_attend(idx, page_table, q.reshape(hd), k_s.reshape(hd), v_s.reshape(hd),
                            cache_k, cache_v)
    lru_o, h_s = _lru_step(x_lru, y_lru, jnp.swapaxes(state_conv, 0, 1), state_rglru, *lru_p)
    y_sample = finish(xs, attn_s.reshape(bd, attn_w), lru_o, ga, gb).reshape(bd, dec_seq, d_model)
    conv_s = jnp.concatenate([state_conv[:, 1:], x_lru[:, None, :]], axis=1)

    return (y_prompt, y_sample,
            k_p.reshape(bp, seq, n_heads, head_dim), v_p.reshape(bp, seq, n_heads, head_dim),
            ki_p.reshape(bp, seq, idx_dim), conv_p, h_p.reshape(bp, lru_w),
            k_s.reshape(bd, dec_seq, n_heads, head_dim), v_s.reshape(bd, dec_seq, n_heads, head_dim),
            ki_s.reshape(bd, dec_seq, idx_dim), conv_s, h_s)
```

```python
import functools

import jax
import jax.numpy as jnp
import numpy as np
from jax import lax
from jax.experimental import pallas as pl
from jax.experimental.pallas import tpu as pltpu

TOPK_MAX = 256
LRU_C = 8.0
EPS = 1e-6
MASK_VALUE = -1e30
LOG2_E = 1.4426950408889634

V7X_VMEM_BYTES = 64 * 1024 * 1024
VMEM_LIMIT_BYTES = 60 * 1024 * 1024
LANES = 128
SUBLANES = 8

BF16 = jnp.bfloat16
F32 = jnp.float32
I32 = jnp.int32
I16 = jnp.int16
INT16_MIN = -(2 ** 15)
INT16_MAX = 2 ** 15 - 1

_NT_DIMS = (((1,), (1,)), ((), ()))


def _params(*sem):
    return pltpu.CompilerParams(dimension_semantics=sem, vmem_limit_bytes=VMEM_LIMIT_BYTES)


def _resident(block_shape, index_map):
    return pl.BlockSpec(block_shape, index_map, pipeline_mode=pl.Buffered(1))


def _row_block(m, target):
    tm = min(m, target)
    assert m % tm == 0, (m, tm)
    return tm


def _rmsnorm_kernel(x_ref, g_ref, o_ref):
    x = x_ref[...]
    y = x * lax.rsqrt(jnp.mean(x * x, axis=-1, keepdims=True) + EPS) * g_ref[...]
    o_ref[...] = y.astype(o_ref.dtype)


def _rmsnorm_bf16(x, g):
    m, d = x.shape
    tm = _row_block(m, 512)
    return pl.pallas_call(
        _rmsnorm_kernel,
        grid=(m // tm,),
        in_specs=[pl.BlockSpec((tm, d), lambda i: (i, 0)), pl.BlockSpec((1, d), lambda i: (0, 0))],
        out_specs=pl.BlockSpec((tm, d), lambda i: (i, 0)),
        out_shape=jax.ShapeDtypeStruct((m, d), BF16),
        compiler_params=_params("parallel"),
        name="rmsnorm_bf16",
    )(x, g.reshape(1, d))


def _matmul_kernel(a_ref, w_ref, *o_refs, out_scale):
    acc = jnp.dot(a_ref[...], w_ref[...], preferred_element_type=F32)
    if out_scale is not None:
        acc = acc * out_scale
    for o_ref in o_refs:
        o_ref[...] = acc.astype(o_ref.dtype)


def _matmul(a, w, out_dtypes, out_scale=None):
    m, k = a.shape
    n = w.shape[1]
    tm = _row_block(m, 512)
    tn = _row_block(n, 1024)
    outs = pl.pallas_call(
        functools.partial(_matmul_kernel, out_scale=out_scale),
        grid=(n // tn, m // tm),
        in_specs=[pl.BlockSpec((tm, k), lambda j, i: (i, 0)), pl.BlockSpec((k, tn), lambda j, i: (0, j))],
        out_specs=[pl.BlockSpec((tm, tn), lambda j, i: (i, j)) for _ in out_dtypes],
        out_shape=[jax.ShapeDtypeStruct((m, n), dt) for dt in out_dtypes],
        compiler_params=_params("parallel", "parallel"),
        name="proj_matmul",
    )(a, w)
    return outs


def _sortable_key(x):
    x = jnp.where(x == 0.0, 0.0, x)
    bits = pltpu.bitcast(x, I32)
    return bits ^ ((bits >> 31) & jnp.int32(0x7FFFFFFF))


def _kth_largest_key(count_ge, k, shape):
    int_min = jnp.int32(-(2**31))
    zero = jnp.zeros(shape, I32)
    ans = jnp.where(count_ge(zero) >= k, zero, jnp.full(shape, int_min, I32))

    def body(it, ans):
        cand = ans | (jnp.int32(1) << (jnp.int32(30) - it))
        return jnp.where(count_ge(cand) >= k, cand, ans)

    return lax.fori_loop(0, 31, body, ans)


def _kth_largest_half(count_ge, k, shape):
    zero = jnp.zeros(shape, I32)
    ans = jnp.where(count_ge(zero) >= k, zero, jnp.full(shape, INT16_MIN, I32))

    def body(it, ans):
        cand = ans | (jnp.int32(1) << (jnp.int32(14) - it))
        return jnp.where(count_ge(cand) >= k, cand, ans)

    return lax.fori_loop(0, 15, body, ans)


IDX_SUB = 256
KV_CHUNK = 512
ATT_CHUNK = 256


def _prompt_attn_kernel(q_ref, qi_ref, kiwi_ref, ki_ref, k_ref, v_ref, o_ref,
                        wb_scr, keys_scr, half_scr, thr_scr, bias_scr, m_scr, l_scr, acc_scr,
                        *, tq, n_heads, head_dim, n_idx_heads, idx_dim, n_sel):
    i = pl.program_id(0)
    row0 = i * tq
    n_chunks = (row0 + tq + KV_CHUNK - 1) // KV_CHUNK
    idx_scale = (idx_dim ** -0.5) * (n_idx_heads ** -0.5)
    rows = row0 + lax.broadcasted_iota(I32, (tq, 1), 0)

    wi = kiwi_ref[:, idx_dim:idx_dim + n_idx_heads] * idx_scale
    for h in range(n_idx_heads):
        wb_scr[h] = jnp.broadcast_to(wi[:, h:h + 1], (tq, LANES))

    def score_chunk(c, carry):
        for sub in range(KV_CHUNK // IDX_SUB):
            col0 = pl.multiple_of(c * KV_CHUNK + sub * IDX_SUB, IDX_SUB)
            ki_c = ki_ref[pl.ds(col0, IDX_SUB), :]
            acc = jnp.zeros((tq, IDX_SUB), F32)
            for h in range(n_idx_heads):
                s = lax.dot_general(qi_ref[:, h * idx_dim:(h + 1) * idx_dim], ki_c, _NT_DIMS,
                                    preferred_element_type=F32)
                wb = wb_scr[h]
                acc = acc + jnp.maximum(s, 0.0) * jnp.concatenate([wb] * (IDX_SUB // LANES), axis=1)
            cols = col0 + lax.broadcasted_iota(I32, (1, IDX_SUB), 1)
            acc = jnp.where(cols <= rows, acc, -jnp.inf)
            key = _sortable_key(acc)
            keys_scr[:, pl.ds(col0, IDX_SUB)] = key
            half_scr[:, pl.ds(col0, IDX_SUB)] = (key >> 16).astype(I16)
        return carry

    lax.fori_loop(0, n_chunks, score_chunk, 0)

    lane_reps = KV_CHUNK // LANES

    def count_ge16(cand):
        cand_b = jnp.broadcast_to(cand.astype(I16), (tq, LANES))

        def body(c, acc):
            col0 = pl.multiple_of(c * KV_CHUNK, KV_CHUNK)
            for j in range(lane_reps):
                blk = half_scr[:, pl.ds(col0 + j * LANES, LANES)]
                acc = acc + jnp.where(blk >= cand_b, jnp.int16(1), jnp.int16(0))
            return acc

        acc = lax.fori_loop(0, n_chunks, body, jnp.zeros((tq, LANES), I16))
        return jnp.sum(acc.astype(F32), axis=1, keepdims=True).astype(I32)

    thr_hi = _kth_largest_half(count_ge16, n_sel, (tq, 1))
    n_above = jnp.where(thr_hi >= INT16_MAX, 0, count_ge16(jnp.minimum(thr_hi + 1, INT16_MAX)))

    def keep_low_halves(c, carry):
        col0 = pl.multiple_of(c * KV_CHUNK, KV_CHUNK)
        key = keys_scr[:, pl.ds(col0, KV_CHUNK)]
        low = (key & 0xFFFF) + INT16_MIN
        half_scr[:, pl.ds(col0, KV_CHUNK)] = jnp.where((key >> 16) == thr_hi, low, INT16_MIN).astype(I16)
        return carry

    lax.fori_loop(0, n_chunks, keep_low_halves, 0)
    thr_lo = _kth_largest_half(count_ge16, n_sel - n_above, (tq, 1))
    thr = (thr_hi << 16) | (thr_lo - INT16_MIN)
    thr_scr[...] = jnp.broadcast_to(thr, (tq, LANES))

    m_scr[...] = jnp.full(m_scr.shape, MASK_VALUE, F32)
    l_scr[...] = jnp.zeros(l_scr.shape, F32)
    acc_scr[...] = jnp.zeros(acc_scr.shape, F32)

    att_reps = ATT_CHUNK // LANES

    def attend_chunk(c, carry):
        col0 = pl.multiple_of(c * ATT_CHUNK, ATT_CHUNK)
        cols = col0 + lax.broadcasted_iota(I32, (1, ATT_CHUNK), 1)
        thr_b = jnp.concatenate([thr_scr[...]] * att_reps, axis=1)
        sel = (keys_scr[:, pl.ds(col0, ATT_CHUNK)] >= thr_b) & (cols <= rows)
        bias_scr[...] = jnp.where(sel, 0.0, MASK_VALUE)
        for h in range(n_heads):
            hs = slice(h * head_dim, (h + 1) * head_dim)
            s = lax.dot_general(q_ref[:, hs], k_ref[pl.ds(col0, ATT_CHUNK), hs], _NT_DIMS,
                                preferred_element_type=F32) + bias_scr[...]
            m_prev = m_scr[h]
            m_new = jnp.maximum(m_prev, jnp.max(s, axis=1, keepdims=True))
            alpha = jnp.exp2(m_prev - m_new)
            p = jnp.exp2(s - jnp.concatenate([m_new] * att_reps, axis=1))
            l_scr[h] = alpha * l_scr[h] + jnp.sum(p, axis=1, keepdims=True)
            acc_scr[h] = alpha * acc_scr[h] + jnp.dot(p.astype(BF16), v_ref[pl.ds(col0, ATT_CHUNK), hs],
                                                      preferred_element_type=F32)
            m_scr[h] = m_new
        return carry

    lax.fori_loop(0, (row0 + tq + ATT_CHUNK - 1) // ATT_CHUNK, attend_chunk, 0)

    for h in range(n_heads):
        o_ref[:, h * head_dim:(h + 1) * head_dim] = (acc_scr[h] / l_scr[h]).astype(o_ref.dtype)


def _prompt_attention(q, qi, kiwi, ki, k, v, *, n_heads, head_dim, n_idx_heads, idx_dim):
    t = q.shape[0]
    tq = _row_block(t, 256)
    assert t % KV_CHUNK == 0 and t >= TOPK_MAX * 2
    n_sel = min(TOPK_MAX, t // 4)
    aw = n_heads * head_dim
    kern = functools.partial(_prompt_attn_kernel, tq=tq, n_heads=n_heads, head_dim=head_dim,
                             n_idx_heads=n_idx_heads, idx_dim=idx_dim, n_sel=n_sel)
    return pl.pallas_call(
        kern,
        grid=(t // tq,),
        in_specs=[
            pl.BlockSpec((tq, aw), lambda i: (i, 0)),
            pl.BlockSpec((tq, n_idx_heads * idx_dim), lambda i: (i, 0)),
            pl.BlockSpec((tq, LANES), lambda i: (i, 0)),
            _resident((t, idx_dim), lambda i: (0, 0)),
            _resident((t, aw), lambda i: (0, 0)),
            _resident((t, aw), lambda i: (0, 0)),
        ],
        out_specs=pl.BlockSpec((tq, aw), lambda i: (i, 0)),
        out_shape=jax.ShapeDtypeStruct((t, aw), BF16),
        scratch_shapes=[
            pltpu.VMEM((n_idx_heads, tq, LANES), F32),
            pltpu.VMEM((tq, t), I32),
            pltpu.VMEM((tq, t), I16),
            pltpu.VMEM((tq, LANES), I32),
            pltpu.VMEM((tq, ATT_CHUNK), F32),
            pltpu.VMEM((n_heads, tq, LANES), F32),
            pltpu.VMEM((n_heads, tq, LANES), F32),
            pltpu.VMEM((n_heads, tq, head_dim), F32),
        ],
        compiler_params=_params("arbitrary"),
        name="prompt_sparse_attention",
    )(q, qi, kiwi, ki, k, v)


def _softplus(x):
    return jnp.maximum(x, 0.0) + jnp.log1p(jnp.exp(-jnp.abs(x)))


def _split_bf16(x):
    hi = x.astype(BF16)
    lo = (x - hi.astype(F32)).astype(BF16)
    return hi, lo


def _lru_gates(xc, wa_ref, ba_ref, wx_ref, bx_ref, lam_ref):
    n_blocks, blk, _ = wa_ref.shape
    ra, rx = [], []
    for n in range(n_blocks):
        x_hi, x_lo = _split_bf16(xc[:, n * blk:(n + 1) * blk])
        for w_ref, dst in ((wa_ref, ra), (wx_ref, rx)):
            w_hi, w_lo = _split_bf16(w_ref[n])
            dst.append(jnp.dot(x_hi, w_hi, preferred_element_type=F32)
                       + jnp.dot(x_hi, w_lo, preferred_element_type=F32)
                       + jnp.dot(x_lo, w_hi, preferred_element_type=F32))
    r = jax.nn.sigmoid(jnp.concatenate(ra, axis=1) + ba_ref[...])
    gate_i = jax.nn.sigmoid(jnp.concatenate(rx, axis=1) + bx_ref[...])
    log_a = (-LRU_C) * r * _softplus(-lam_ref[...])
    a = jnp.exp(log_a)
    b = jnp.sqrt(jnp.tanh(-log_a) * (1.0 + a * a)) * (gate_i * xc)
    return a, b


def _lru_prompt_kernel(x_ref, y_ref, cw_ref, cb_ref, wa_ref, ba_ref, wx_ref, bx_ref, lam_ref,
                       o_ref, hlast_ref, ext_scr, a_scr, b_scr, h_scr, *, tb, conv_width):
    t = pl.program_id(0)
    pad = SUBLANES
    assert conv_width - 1 <= pad

    @pl.when(t == 0)
    def _():
        ext_scr[0:pad, :] = jnp.zeros((pad, ext_scr.shape[1]), F32)
        hlast_ref[...] = jnp.zeros(hlast_ref.shape, F32)

    ext_scr[pad:pad + tb, :] = x_ref[...]
    conv = None
    for j in range(conv_width):
        term = cw_ref[j:j + 1, :] * ext_scr[pl.ds(pad - (conv_width - 1) + j, tb), :]
        conv = term if conv is None else conv + term
    xc = cb_ref[...] + conv
    a, b = _lru_gates(xc, wa_ref, ba_ref, wx_ref, bx_ref, lam_ref)
    a_scr[...] = a
    b_scr[...] = b

    sub = lax.broadcasted_iota(I32, (SUBLANES, 1), 0)

    def tile(j, h_prev):
        r0 = pl.multiple_of(j * SUBLANES, SUBLANES)
        at = a_scr[pl.ds(r0, SUBLANES), :]
        bt = b_scr[pl.ds(r0, SUBLANES), :]
        for d in (1, 2, 4):
            a_sh = pltpu.roll(at, d, 0)
            b_sh = pltpu.roll(bt, d, 0)
            ok = sub >= d
            bt = jnp.where(ok, at * b_sh + bt, bt)
            at = jnp.where(ok, at * a_sh, at)
        h = at * h_prev + bt
        h_scr[pl.ds(r0, SUBLANES), :] = h
        return jnp.broadcast_to(h[SUBLANES - 1:SUBLANES, :], h.shape)

    h_last = lax.fori_loop(0, tb // SUBLANES, tile, hlast_ref[...])
    hlast_ref[...] = h_last
    ext_scr[0:pad, :] = ext_scr[tb:tb + pad, :]
    o_ref[...] = (h_scr[...] * jax.nn.gelu(y_ref[...])).astype(o_ref.dtype)


def _lru_prompt(x_lru, y_lru, conv_w, conv_b, rg_wa, rg_ba, rg_wx, rg_bx, rg_lambda):
    t, w = x_lru.shape
    tb = _row_block(t, 256)
    cw = conv_w.shape[0]
    vec = lambda a: a.reshape(1, w)
    full = lambda shape: pl.BlockSpec(shape, lambda i: (0,) * len(shape))
    out, h_last = pl.pallas_call(
        functools.partial(_lru_prompt_kernel, tb=tb, conv_width=cw),
        grid=(t // tb,),
        in_specs=[pl.BlockSpec((tb, w), lambda i: (i, 0)), pl.BlockSpec((tb, w), lambda i: (i, 0)),
                  full((cw, w)), full((1, w)), full(rg_wa.shape), full((1, w)), full(rg_wx.shape),
                  full((1, w)), full((1, w))],
        out_specs=[pl.BlockSpec((tb, w), lambda i: (i, 0)), full((SUBLANES, w))],
        out_shape=[jax.ShapeDtypeStruct((t, w), BF16), jax.ShapeDtypeStruct((SUBLANES, w), F32)],
        scratch_shapes=[pltpu.VMEM((tb + SUBLANES, w), F32), pltpu.VMEM((tb, w), F32),
                        pltpu.VMEM((tb, w), F32), pltpu.VMEM((tb, w), F32)],
        compiler_params=_params("arbitrary"),
        name="rglru_prompt",
    )(x_lru, y_lru, conv_w, vec(conv_b), rg_wa, vec(rg_ba), rg_wx, vec(rg_bx), vec(rg_lambda))
    return out, h_last[0]


def _lru_step_kernel(x_ref, y_ref, sc_ref, h0_ref, cw_ref, cb_ref, wa_ref, ba_ref, wx_ref, bx_ref,
                     lam_ref, o_ref, h_ref, *, conv_width):
    conv = None
    for j in range(conv_width):
        src = x_ref[...] if j == conv_width - 1 else sc_ref[j]
        term = cw_ref[j:j + 1, :] * src
        conv = term if conv is None else conv + term
    xc = cb_ref[...] + conv
    a, b = _lru_gates(xc, wa_ref, ba_ref, wx_ref, bx_ref, lam_ref)
    h = b + a * h0_ref[...]
    h_ref[...] = h
    o_ref[...] = (h * jax.nn.gelu(y_ref[...])).astype(o_ref.dtype)


def _lru_step(x_lru, y_lru, state_conv_t, h0, conv_w, conv_b, rg_wa, rg_ba, rg_wx, rg_bx, rg_lambda):
    b, w = x_lru.shape
    vec = lambda a: a.reshape(1, w)
    return pl.pallas_call(
        functools.partial(_lru_step_kernel, conv_width=conv_w.shape[0]),
        out_shape=[jax.ShapeDtypeStruct((b, w), BF16), jax.ShapeDtypeStruct((b, w), F32)],
        compiler_params=pltpu.CompilerParams(vmem_limit_bytes=VMEM_LIMIT_BYTES),
        name="rglru_step",
    )(x_lru, y_lru, state_conv_t, h0, conv_w, vec(conv_b), rg_wa, vec(rg_ba), rg_wx, vec(rg_bx),
      vec(rg_lambda))


def _merge_kernel(x_ref, ao_ref, lo_ref, ga_ref, gb_ref, wba_ref, wbl_ref, wout_ref, g2_ref,
                  x1_ref, h2_ref):
    attn = jnp.dot(ao_ref[...], wba_ref[...], preferred_element_type=F32)
    lru = jnp.dot(lo_ref[...], wbl_ref[...], preferred_element_type=F32)
    merged = jax.nn.sigmoid(ga_ref[...]) * attn + jax.nn.sigmoid(gb_ref[...]) * lru
    x1 = x_ref[...] + jnp.dot(merged.astype(BF16), wout_ref[...], preferred_element_type=F32)
    x1_ref[...] = x1
    h2 = x1 * lax.rsqrt(jnp.mean(x1 * x1, axis=-1, keepdims=True) + EPS) * g2_ref[...]
    h2_ref[...] = h2.astype(h2_ref.dtype)


def _merge(x, attn_o, lru_o, ga, gb, w_ba, w_bl, w_out, norm2_g):
    m, d = x.shape
    tm = _row_block(m, 256)
    row = lambda width: pl.BlockSpec((tm, width), lambda i: (i, 0))
    return pl.pallas_call(
        _merge_kernel,
        grid=(m // tm,),
        in_specs=[row(d), row(attn_o.shape[1]), row(lru_o.shape[1]), row(d), row(d),
                  _resident(w_ba.shape, lambda i: (0, 0)), _resident(w_bl.shape, lambda i: (0, 0)),
                  _resident(w_out.shape, lambda i: (0, 0)), pl.BlockSpec((1, d), lambda i: (0, 0))],
        out_specs=[row(d), row(d)],
        out_shape=[jax.ShapeDtypeStruct((m, d), F32), jax.ShapeDtypeStruct((m, d), BF16)],
        compiler_params=_params("parallel"),
        name="gated_merge",
    )(x, attn_o, lru_o, ga, gb, w_ba, w_bl, w_out, norm2_g.reshape(1, d))


def _ffn_kernel(h_ref, x1_ref, wg_ref, wu_ref, wd_ref, gf_ref, y_ref, acc_scr):
    f = pl.program_id(1)

    @pl.when(f == 0)
    def _():
        acc_scr[...] = jnp.zeros(acc_scr.shape, F32)

    h = h_ref[...]
    gate = jnp.dot(h, wg_ref[...], preferred_element_type=F32)
    up = jnp.dot(h, wu_ref[...], preferred_element_type=F32)
    act = (jax.nn.silu(gate) * up).astype(BF16)
    acc_scr[...] += jnp.dot(act, wd_ref[...], preferred_element_type=F32)

    @pl.when(f == pl.num_programs(1) - 1)
    def _():
        x2 = x1_ref[...] + acc_scr[...]
        y_ref[...] = x2 * lax.rsqrt(jnp.mean(x2 * x2, axis=-1, keepdims=True) + EPS) * gf_ref[...]


def _ffn(h2, x1, w_gate, w_up, w_down, norm_f_g):
    m, d = x1.shape
    ff = w_gate.shape[1]
    tm = _row_block(m, 512)
    tf = 512
    assert ff % tf == 0
    return pl.pallas_call(
        _ffn_kernel,
        grid=(m // tm, ff // tf),
        in_specs=[pl.BlockSpec((tm, d), lambda i, f: (i, 0)), pl.BlockSpec((tm, d), lambda i, f: (i, 0)),
                  pl.BlockSpec((d, tf), lambda i, f: (0, f)), pl.BlockSpec((d, tf), lambda i, f: (0, f)),
                  pl.BlockSpec((tf, d), lambda i, f: (f, 0)), pl.BlockSpec((1, d), lambda i, f: (0, 0))],
        out_specs=pl.BlockSpec((tm, d), lambda i, f: (i, 0)),
        out_shape=jax.ShapeDtypeStruct((m, d), F32),
        scratch_shapes=[pltpu.VMEM((tm, d), F32)],
        compiler_params=_params("parallel", "arbitrary"),
        name="swiglu_ffn",
    )(h2, x1, w_gate, w_up, w_down, norm_f_g.reshape(1, d))


SCORE_PAGES = 16


def _sample_score_kernel(pt_ref, qi_ref, wi_ref, kin_ref, kidx_hbm, keys_ref, keynew_ref,
                         kibuf, sems, *, n_pages, page, n_idx_heads, idx_dim):
    b = pl.program_id(0)
    slot = b % 2
    idx_scale = (idx_dim ** -0.5) * (n_idx_heads ** -0.5)

    def page_copy(seq, sl, p):
        return pltpu.make_async_copy(kidx_hbm.at[pt_ref[seq, p]], kibuf.at[sl, p], sems.at[sl])

    def start_all(seq, sl):
        def body(p, c):
            page_copy(seq, sl, p).start()
            return c

        lax.fori_loop(0, n_pages, body, 0)

    @pl.when(b == 0)
    def _():
        start_all(0, 0)

    @pl.when(b + 1 < pl.num_programs(0))
    def _():
        start_all(b + 1, 1 - slot)

    def wait(p, c):
        page_copy(b, slot, p).wait()
        return c

    lax.fori_loop(0, n_pages, wait, 0)

    qi = qi_ref[0]
    wcol = wi_ref[0] * idx_scale
    wb = jnp.broadcast_to(wcol, (n_idx_heads, page))

    def score(g, c):
        p0 = pl.multiple_of(g * SCORE_PAGES, SCORE_PAGES)
        pages = kibuf[slot, pl.ds(p0, SCORE_PAGES)]
        kblk = jnp.concatenate([pages[j] for j in range(SCORE_PAGES)], axis=1).astype(BF16)
        s = jnp.dot(qi, kblk, preferred_element_type=F32)
        rows = []
        for j in range(SCORE_PAGES):
            sj = jnp.maximum(s[:, j * page:(j + 1) * page], 0.0) * wb
            rows.append(jnp.sum(sj, axis=0, keepdims=True))
        keys_ref[0, pl.ds(p0, SCORE_PAGES), :] = _sortable_key(jnp.concatenate(rows, axis=0))
        return c

    lax.fori_loop(0, n_pages // SCORE_PAGES, score, 0)

    s_new = jnp.sum(qi.astype(F32) * kin_ref[0].astype(BF16).astype(F32), axis=1, keepdims=True)
    i_new = jnp.sum(jnp.maximum(s_new, 0.0) * wcol, axis=0, keepdims=True)
    keynew_ref[0] = jnp.broadcast_to(_sortable_key(i_new), (1, LANES))


def _sample_pick_kernel(keys_ref, keynew_ref, pt_ref, rows_ref, sel_scr, selnew_scr,
                        *, n_pages, page, n_sel):
    bsz = keys_ref.shape[0]
    key_new = keynew_ref[...][:, :, :1]

    def count_ge(cand):
        cnt = jnp.sum(jnp.where(keys_ref[...] >= cand, 1.0, 0.0), axis=1, keepdims=True)
        cnt = jnp.sum(cnt, axis=2, keepdims=True)
        return cnt + jnp.where(key_new >= cand, 1.0, 0.0)

    thr = _kth_largest_key(count_ge, n_sel, (bsz, 1, 1))
    sel_scr[...] = jnp.where(keys_ref[...] >= thr, 1.0, 0.0)
    selnew_scr[...] = jnp.broadcast_to(jnp.where(key_new >= thr, 1, 0), selnew_scr.shape)

    tri_o = jnp.where(lax.broadcasted_iota(I32, (page, page), 1) <= lax.broadcasted_iota(I32, (page, page), 0),
                      1.0, 0.0).astype(BF16)
    tri_p = jnp.where(lax.broadcasted_iota(I32, (n_pages, n_pages), 1)
                      <= lax.broadcasted_iota(I32, (n_pages, n_pages), 0), 1.0, 0.0).astype(BF16)
    j_row = lax.broadcasted_iota(I32, (1, n_sel), 1).astype(F32)
    p_col = lax.broadcasted_iota(I32, (n_pages, 1), 0).astype(F32)
    reps = n_sel // LANES

    def invert(b, c):
        sel = sel_scr[b]
        cnt = jnp.sum(sel, axis=1, keepdims=True)
        cnt_b = jnp.broadcast_to(cnt, (n_pages, LANES))
        cum_b = jnp.dot(tri_p, cnt_b.astype(BF16), preferred_element_type=F32)
        base_b = cum_b - cnt_b
        tile = lambda a: jnp.concatenate([a] * reps, axis=1)
        page_of = jnp.sum(jnp.where(tile(cum_b) <= j_row, 1.0, 0.0), axis=0, keepdims=True)
        onehot = jnp.where(p_col == page_of, 1.0, 0.0)
        base_j = jnp.sum(onehot * tile(base_b), axis=0, keepdims=True)
        pt_col = pt_ref[b].astype(F32)
        phys_j = jnp.sum(onehot * pt_col, axis=0, keepdims=True)
        incl_t = lax.dot_general(tri_o, sel.astype(BF16), _NT_DIMS, preferred_element_type=F32)
        incl_j = jnp.dot(incl_t.astype(BF16), onehot.astype(BF16), preferred_element_type=F32)
        off_j = jnp.sum(jnp.where(incl_j <= j_row - base_j, 1.0, 0.0), axis=0, keepdims=True)
        row = (phys_j * page + off_j).astype(I32)
        total = jnp.sum(cnt, axis=0, keepdims=True)
        is_new = (selnew_scr[b][:1, :1] > 0) & (j_row == n_sel - 1)
        rows_ref[pl.ds(b, 1), :] = jnp.where(is_new | (j_row >= total), -1, row)
        return c

    lax.fori_loop(0, bsz, invert, 0)


def _sample_select(qi, wi, ki_new, cache_kidx, page_table, *, n_sel):
    bsz, n_idx_heads, idx_dim = qi.shape
    n_pages = page_table.shape[1]
    n_pool, page, _ = cache_kidx.shape
    assert page == LANES and n_pages % SCORE_PAGES == 0 and n_sel % LANES == 0
    assert n_pool * page < 2 ** 24
    kidx_t = jnp.swapaxes(cache_kidx, 1, 2)
    keys, key_new = pl.pallas_call(
        functools.partial(_sample_score_kernel, n_pages=n_pages, page=page, n_idx_heads=n_idx_heads,
                          idx_dim=idx_dim),
        grid_spec=pltpu.PrefetchScalarGridSpec(
            num_scalar_prefetch=1,
            grid=(bsz,),
            in_specs=[pl.BlockSpec((1, n_idx_heads, idx_dim), lambda b, pt: (b, 0, 0)),
                      pl.BlockSpec((1, n_idx_heads, 1), lambda b, pt: (b, 0, 0)),
                      pl.BlockSpec((1, 1, idx_dim), lambda b, pt: (b, 0, 0)),
                      pl.BlockSpec(memory_space=pl.ANY)],
            out_specs=[pl.BlockSpec((1, n_pages, page), lambda b, pt: (b, 0, 0)),
                       pl.BlockSpec((1, 1, LANES), lambda b, pt: (b, 0, 0))],
            scratch_shapes=[pltpu.VMEM((2, n_pages, idx_dim, page), F32), pltpu.SemaphoreType.DMA((2,))],
        ),
        out_shape=[jax.ShapeDtypeStruct((bsz, n_pages, page), I32),
                   jax.ShapeDtypeStruct((bsz, 1, LANES), I32)],
        compiler_params=_params("arbitrary"),
        name="sample_index_scores",
    )(page_table, qi, wi, ki_new, kidx_t)
    return pl.pallas_call(
        functools.partial(_sample_pick_kernel, n_pages=n_pages, page=page, n_sel=n_sel),
        out_shape=jax.ShapeDtypeStruct((bsz, n_sel), I32),
        scratch_shapes=[pltpu.VMEM((bsz, n_pages, page), F32), pltpu.VMEM((bsz, SUBLANES, LANES), I32)],
        compiler_params=pltpu.CompilerParams(vmem_limit_bytes=VMEM_LIMIT_BYTES),
        name="sample_index_pick",
    )(keys, key_new, page_table[:, :, None])


def _sample_attend_kernel(rows_sm, rowsv_ref, q_ref, kn_ref, vn_ref, ck_hbm, cv_hbm, o_ref,
                          kbuf, vbuf, sems, *, n_sel, n_heads, head_dim):
    b = pl.program_id(0)
    slot = b % 2

    def copies(seq, sl, j):
        row = jnp.maximum(rows_sm[seq, j], 0)
        return (pltpu.make_async_copy(ck_hbm.at[row], kbuf.at[sl, j], sems.at[0, sl]),
                pltpu.make_async_copy(cv_hbm.at[row], vbuf.at[sl, j], sems.at[1, sl]))

    def start_all(seq, sl):
        def body(j, c):
            ck, cv = copies(seq, sl, j)
            ck.start()
            cv.start()
            return c

        lax.fori_loop(0, n_sel, body, 0)

    @pl.when(b == 0)
    def _():
        start_all(0, 0)

    @pl.when(b + 1 < pl.num_programs(0))
    def _():
        start_all(b + 1, 1 - slot)

    def wait(j, c):
        ck, cv = copies(b, slot, j)
        ck.wait()
        cv.wait()
        return c

    lax.fori_loop(0, n_sel, wait, 0)

    from_new = rowsv_ref[0] < 0
    for h in range(n_heads):
        k_h = jnp.where(from_new, kn_ref[0, h:h + 1, :], kbuf[slot, :, h, :])
        v_h = jnp.where(from_new, vn_ref[0, h:h + 1, :], vbuf[slot, :, h, :])
        q_h = q_ref[0, h:h + 1, :]
        s = lax.dot_general(q_h, k_h.astype(BF16), _NT_DIMS, preferred_element_type=F32)
        m = jnp.max(s, axis=1, keepdims=True)
        p = jnp.exp2(s - m)
        o = jnp.dot(p.astype(BF16), v_h.astype(BF16), preferred_element_type=F32)
        o_ref[0, h:h + 1, :] = (o / jnp.sum(p, axis=1, keepdims=True)).astype(o_ref.dtype)


def _sample_attend(rows, q, k_new, v_new, cache_k, cache_v):
    bsz, n_sel = rows.shape
    n_pool, page, n_heads, head_dim = cache_k.shape
    kern = functools.partial(_sample_attend_kernel, n_sel=n_sel, n_heads=n_heads, head_dim=head_dim)
    per_seq = lambda: pl.BlockSpec((1, n_heads, head_dim), lambda b, r_sm: (b, 0, 0))
    flat = lambda c: c.reshape(n_pool * page, n_heads, head_dim)
    return pl.pallas_call(
        kern,
        grid_spec=pltpu.PrefetchScalarGridSpec(
            num_scalar_prefetch=1,
            grid=(bsz,),
            in_specs=[pl.BlockSpec((1, n_sel, 1), lambda b, r_sm: (b, 0, 0)),
                      per_seq(), per_seq(), per_seq(),
                      pl.BlockSpec(memory_space=pl.ANY), pl.BlockSpec(memory_space=pl.ANY)],
            out_specs=per_seq(),
            scratch_shapes=[pltpu.VMEM((2, n_sel, n_heads, head_dim), F32),
                            pltpu.VMEM((2, n_sel, n_heads, head_dim), F32),
                            pltpu.SemaphoreType.DMA((2, 2))],
        ),
        out_shape=jax.ShapeDtypeStruct((bsz, n_heads, head_dim), BF16),
        compiler_params=_params("arbitrary"),
        name="sample_sparse_attention",
    )(rows, rows[:, :, None], q, k_new, v_new, flat(cache_k), flat(cache_v))


def kernel(x_prompt, x_sample, cache_k, cache_v, cache_kidx, state_conv, state_rglru, page_table,
           norm1_g, w_in, conv_w, conv_b, rg_wa, rg_ba, rg_wx, rg_bx, rg_lambda,
           w_branch_attn, w_branch_lru, w_out, norm2_g, w_ffn_gate, w_ffn_up, w_ffn_down, norm_f_g):
    bp, seq, d_model = x_prompt.shape
    bd, dec_seq, _ = x_sample.shape
    _, page, n_heads, head_dim = cache_k.shape
    idx_dim = cache_kidx.shape[2]
    lru_w = conv_w.shape[1]
    conv_width = conv_w.shape[0]
    attn_w = n_heads * head_dim
    n_idx_heads = (w_in.shape[1] - 3 * attn_w - idx_dim - 2 * lru_w - 2 * d_model) // (idx_dim + 1)
    assert bp == 1 and dec_seq == 1
    assert idx_dim + n_idx_heads <= LANES

    sizes = [attn_w, attn_w, attn_w, n_idx_heads * idx_dim, idx_dim + n_idx_heads, lru_w, lru_w,
             d_model, d_model]
    cuts = np.cumsum([0] + sizes)
    w_q, w_k, w_v, w_qi, w_kiwi, w_xl, w_yl, w_ga, w_gb = (
        w_in[:, cuts[n]:cuts[n + 1]].astype(BF16) for n in range(len(sizes)))
    w_kiwi = jnp.pad(w_kiwi, ((0, 0), (0, LANES - w_kiwi.shape[1])))
    tail_w = (w_branch_attn.astype(BF16), w_branch_lru.astype(BF16), w_out.astype(BF16), norm2_g)
    ffn_w = (w_ffn_gate.astype(BF16), w_ffn_up.astype(BF16), w_ffn_down.astype(BF16), norm_f_g)
    lru_p = (conv_w, conv_b, rg_wa, rg_ba, rg_wx, rg_bx, rg_lambda)

    def project(x2d):
        h = _rmsnorm_bf16(x2d, norm1_g)
        (q,) = _matmul(h, w_q, [BF16], out_scale=head_dim ** -0.5 * LOG2_E)
        k, k_bf = _matmul(h, w_k, [F32, BF16])
        v, v_bf = _matmul(h, w_v, [F32, BF16])
        (qi,) = _matmul(h, w_qi, [BF16])
        (kiwi,) = _matmul(h, w_kiwi, [F32])
        (x_lru,) = _matmul(h, w_xl, [F32])
        (y_lru,) = _matmul(h, w_yl, [F32])
        (ga,) = _matmul(h, w_ga, [F32])
        (gb,) = _matmul(h, w_gb, [F32])
        return q, k, k_bf, v, v_bf, qi, kiwi, x_lru, y_lru, ga, gb

    def finish(x2d, attn_o, lru_o, ga, gb):
        x1, h2 = _merge(x2d, attn_o, lru_o, ga, gb, *tail_w)
        return _ffn(h2, x1, *ffn_w)

    xp = x_prompt.reshape(seq, d_model)
    q, k_p, k_bf, v_p, v_bf, qi, kiwi, x_lru, y_lru, ga, gb = project(xp)
    ki_p = kiwi[:, :idx_dim]
    attn_p = _prompt_attention(q, qi, kiwi, ki_p.astype(BF16), k_bf, v_bf, n_heads=n_heads,
                               head_dim=head_dim, n_idx_heads=n_idx_heads, idx_dim=idx_dim)
    lru_o, h_p = _lru_prompt(x_lru, y_lru, *lru_p)
    y_prompt = finish(xp, attn_p, lru_o, ga, gb).reshape(bp, seq, d_model)
    conv_p = x_lru[seq - (conv_width - 1):].reshape(bp, conv_width - 1, lru_w)

    xs = x_sample.reshape(bd, d_model)
    q, k_s, _, v_s, _, qi, kiwi, x_lru, y_lru, ga, gb = project(xs)
    ki_s = kiwi[:, :idx_dim]
    wi_s = kiwi[:, idx_dim:idx_dim + n_idx_heads]
    past = page_table.shape[1] * page
    n_sel = min(TOPK_MAX, (past + dec_seq) // 4)
    rows = _sample_select(qi.reshape(bd, n_idx_heads, idx_dim), wi_s[:, :, None], ki_s[:, None, :],
                          cache_kidx, page_table, n_sel=n_sel)
    hd = (bd, n_heads, head_dim)
    attn_s = _sample_attend(rows, q.reshape(hd), k_s.reshape(hd), v_s.reshape(hd), cache_k, cache_v)
    lru_o, h_s = _lru_step(x_lru, y_lru, jnp.swapaxes(state_conv, 0, 1), state_rglru, *lru_p)
    y_sample = finish(xs, attn_s.reshape(bd, attn_w), lru_o, ga, gb).reshape(bd, dec_seq, d_model)
    conv_s = jnp.concatenate([state_conv[:, 1:], x_lru[:, None, :]], axis=1)

    return (y_prompt, y_sample,
            k_p.reshape(bp, seq, n_heads, head_dim), v_p.reshape(bp, seq, n_heads, head_dim),
            ki_p.reshape(bp, seq, idx_dim), conv_p, h_p.reshape(bp, lru_w),
            k_s.reshape(bd, dec_seq, n_heads, head_dim), v_s.reshape(bd, dec_seq, n_heads, head_dim),
            ki_s.reshape(bd, dec_seq, idx_dim), conv_s, h_s)
```

```python
import functools

import jax
import jax.numpy as jnp
import numpy as np
from jax import lax
from jax.experimental import pallas as pl
from jax.experimental.pallas import tpu as pltpu

TOPK_MAX = 256
LRU_C = 8.0
EPS = 1e-6
MASK_VALUE = -1e30
LOG2_E = 1.4426950408889634

V7X_VMEM_BYTES = 64 * 1024 * 1024
VMEM_LIMIT_BYTES = 60 * 1024 * 1024
LANES = 128
SUBLANES = 8

BF16 = jnp.bfloat16
F32 = jnp.float32
I32 = jnp.int32
I16 = jnp.int16
INT16_MIN = -(2 ** 15)
INT16_MAX = 2 ** 15 - 1

_NT_DIMS = (((1,), (1,)), ((), ()))


def _params(*sem):
    return pltpu.CompilerParams(dimension_semantics=sem, vmem_limit_bytes=VMEM_LIMIT_BYTES)


def _resident(block_shape, index_map):
    return pl.BlockSpec(block_shape, index_map, pipeline_mode=pl.Buffered(1))


def _row_block(m, target):
    tm = min(m, target)
    assert m % tm == 0, (m, tm)
    return tm


def _rmsnorm_kernel(x_ref, g_ref, o_ref):
    x = x_ref[...]
    y = x * lax.rsqrt(jnp.mean(x * x, axis=-1, keepdims=True) + EPS) * g_ref[...]
    o_ref[...] = y.astype(o_ref.dtype)


def _rmsnorm_bf16(x, g):
    m, d = x.shape
    tm = _row_block(m, 512)
    return pl.pallas_call(
        _rmsnorm_kernel,
        grid=(m // tm,),
        in_specs=[pl.BlockSpec((tm, d), lambda i: (i, 0)), pl.BlockSpec((1, d), lambda i: (0, 0))],
        out_specs=pl.BlockSpec((tm, d), lambda i: (i, 0)),
        out_shape=jax.ShapeDtypeStruct((m, d), BF16),
        compiler_params=_params("parallel"),
        name="rmsnorm_bf16",
    )(x, g.reshape(1, d))


def _matmul_kernel(a_ref, w_ref, *o_refs, out_scale):
    acc = jnp.dot(a_ref[...], w_ref[...], preferred_element_type=F32)
    if out_scale is not None:
        acc = acc * out_scale
    for o_ref in o_refs:
        if len(o_ref.shape) == 3:
            hd = o_ref.shape[2]
            for h in range(o_ref.shape[0]):
                o_ref[h] = acc[:, h * hd:(h + 1) * hd].astype(o_ref.dtype)
        else:
            o_ref[...] = acc.astype(o_ref.dtype)


def _matmul(a, w, out_dtypes, out_scale=None, head_dim=None):
    m, k = a.shape
    n = w.shape[1]
    tm = _row_block(m, 512)
    tn = _row_block(n, 1024)
    by_head = [head_dim is not None and dt == BF16 for dt in out_dtypes]
    assert head_dim is None or tn % head_dim == 0
    outs = pl.pallas_call(
        functools.partial(_matmul_kernel, out_scale=out_scale),
        grid=(n // tn, m // tm),
        in_specs=[pl.BlockSpec((tm, k), lambda j, i: (i, 0)), pl.BlockSpec((k, tn), lambda j, i: (0, j))],
        out_specs=[pl.BlockSpec((tn // head_dim, tm, head_dim), lambda j, i: (j, i, 0)) if hm
                   else pl.BlockSpec((tm, tn), lambda j, i: (i, j)) for hm in by_head],
        out_shape=[jax.ShapeDtypeStruct((n // head_dim, m, head_dim) if hm else (m, n), dt)
                   for hm, dt in zip(by_head, out_dtypes)],
        compiler_params=_params("parallel", "parallel"),
        name="proj_matmul",
    )(a, w)
    return outs


def _sortable_key(x):
    x = jnp.where(x == 0.0, 0.0, x)
    bits = pltpu.bitcast(x, I32)
    return bits ^ ((bits >> 31) & jnp.int32(0x7FFFFFFF))


def _kth_largest_key(count_ge, k, shape):
    int_min = jnp.int32(-(2**31))
    zero = jnp.zeros(shape, I32)
    ans = jnp.where(count_ge(zero) >= k, zero, jnp.full(shape, int_min, I32))

    def body(it, ans):
        cand = ans | (jnp.int32(1) << (jnp.int32(30) - it))
        return jnp.where(count_ge(cand) >= k, cand, ans)

    return lax.fori_loop(0, 31, body, ans)


def _kth_largest_half(count_ge, k, shape):
    zero = jnp.zeros(shape, I32)
    ans = jnp.where(count_ge(zero) >= k, zero, jnp.full(shape, INT16_MIN, I32))

    def body(it, ans):
        cand = ans | (jnp.int32(1) << (jnp.int32(14) - it))
        return jnp.where(count_ge(cand) >= k, cand, ans)

    return lax.fori_loop(0, 15, body, ans)


IDX_SUB = 256
KV_CHUNK = 512
ATT_CHUNK = 256


def _prompt_attn_kernel(q_ref, qi_ref, kiwi_ref, ki_ref, k_ref, v_ref, o_ref,
                        wb_scr, keys_scr, half_scr, thr_scr, bias_scr, m_scr, l_scr, acc_scr,
                        *, tq, n_heads, head_dim, n_idx_heads, idx_dim, n_sel):
    i = pl.program_id(0)
    row0 = i * tq
    n_chunks = (row0 + tq + KV_CHUNK - 1) // KV_CHUNK
    idx_scale = (idx_dim ** -0.5) * (n_idx_heads ** -0.5)
    rows = row0 + lax.broadcasted_iota(I32, (tq, 1), 0)

    wi = kiwi_ref[:, idx_dim:idx_dim + n_idx_heads] * idx_scale
    for h in range(n_idx_heads):
        wb_scr[h] = jnp.broadcast_to(wi[:, h:h + 1], (tq, LANES))

    lane_reps = KV_CHUNK // LANES

    def score_chunk(c, carry):
        for sub in range(KV_CHUNK // IDX_SUB):
            col0 = pl.multiple_of(c * KV_CHUNK + sub * IDX_SUB, IDX_SUB)
            ki_c = ki_ref[pl.ds(col0, IDX_SUB), :]
            acc = jnp.zeros((tq, IDX_SUB), F32)
            for h in range(n_idx_heads):
                s = lax.dot_general(qi_ref[:, h * idx_dim:(h + 1) * idx_dim], ki_c, _NT_DIMS,
                                    preferred_element_type=F32)
                wb = wb_scr[h]
                acc = acc + jnp.maximum(s, 0.0) * jnp.concatenate([wb] * (IDX_SUB // LANES), axis=1)
            cols = col0 + lax.broadcasted_iota(I32, (1, IDX_SUB), 1)
            acc = jnp.where(cols <= rows, acc, -jnp.inf)
            key = _sortable_key(acc)
            blk0 = c * lane_reps + sub * (IDX_SUB // LANES)
            for j in range(IDX_SUB // LANES):
                key_j = key[:, j * LANES:(j + 1) * LANES]
                keys_scr[blk0 + j] = key_j
                half_scr[blk0 + j] = (key_j >> 16).astype(I16)
        return carry

    lax.fori_loop(0, n_chunks, score_chunk, 0)

    def count_ge16(cand):
        cand_b = jnp.broadcast_to(cand.astype(I16), (tq, LANES))

        def body(c, acc):
            for j in range(lane_reps):
                acc = acc + jnp.where(half_scr[c * lane_reps + j] >= cand_b, jnp.int16(1), jnp.int16(0))
            return acc

        acc = lax.fori_loop(0, n_chunks, body, jnp.zeros((tq, LANES), I16))
        return jnp.sum(acc.astype(F32), axis=1, keepdims=True).astype(I32)

    thr_hi = _kth_largest_half(count_ge16, n_sel, (tq, 1))
    n_above = jnp.where(thr_hi >= INT16_MAX, 0, count_ge16(jnp.minimum(thr_hi + 1, INT16_MAX)))

    def keep_low_halves(c, carry):
        for j in range(lane_reps):
            key = keys_scr[c * lane_reps + j]
            low = (key & 0xFFFF) + INT16_MIN
            half_scr[c * lane_reps + j] = jnp.where((key >> 16) == thr_hi, low, INT16_MIN).astype(I16)
        return carry

    lax.fori_loop(0, n_chunks, keep_low_halves, 0)
    thr_lo = _kth_largest_half(count_ge16, n_sel - n_above, (tq, 1))
    thr = (thr_hi << 16) | (thr_lo - INT16_MIN)
    thr_scr[...] = jnp.broadcast_to(thr, (tq, LANES))

    m_scr[...] = jnp.full(m_scr.shape, MASK_VALUE, F32)
    l_scr[...] = jnp.zeros(l_scr.shape, F32)
    acc_scr[...] = jnp.zeros(acc_scr.shape, F32)

    att_reps = ATT_CHUNK // LANES

    def attend_chunk(c, carry):
        col0 = pl.multiple_of(c * ATT_CHUNK, ATT_CHUNK)
        cols = col0 + lax.broadcasted_iota(I32, (1, ATT_CHUNK), 1)
        thr_b = jnp.concatenate([thr_scr[...]] * att_reps, axis=1)
        keys = jnp.concatenate([keys_scr[c * att_reps + j] for j in range(att_reps)], axis=1)
        sel = (keys >= thr_b) & (cols <= rows)
        bias_scr[...] = jnp.where(sel, 0.0, MASK_VALUE)
        for h in range(n_heads):
            s = lax.dot_general(q_ref[h], k_ref[h, pl.ds(col0, ATT_CHUNK), :], _NT_DIMS,
                                preferred_element_type=F32) + bias_scr[...]
            m_prev = m_scr[h]
            m_new = jnp.maximum(m_prev, jnp.max(s, axis=1, keepdims=True))
            alpha = jnp.exp2(m_prev - m_new)
            p = jnp.exp2(s - jnp.concatenate([m_new] * att_reps, axis=1))
            l_scr[h] = alpha * l_scr[h] + jnp.sum(p, axis=1, keepdims=True)
            acc_scr[h] = alpha * acc_scr[h] + jnp.dot(p.astype(BF16), v_ref[h, pl.ds(col0, ATT_CHUNK), :],
                                                      preferred_element_type=F32)
            m_scr[h] = m_new
        return carry

    lax.fori_loop(0, (row0 + tq + ATT_CHUNK - 1) // ATT_CHUNK, attend_chunk, 0)

    for h in range(n_heads):
        o_ref[:, h * head_dim:(h + 1) * head_dim] = (acc_scr[h] / l_scr[h]).astype(o_ref.dtype)


def _prompt_attention(q, qi, kiwi, ki, k, v, *, n_heads, head_dim, n_idx_heads, idx_dim):
    t = q.shape[1]
    tq = _row_block(t, 256)
    assert t % KV_CHUNK == 0 and t >= TOPK_MAX * 2 and head_dim == LANES
    n_sel = min(TOPK_MAX, t // 4)
    aw = n_heads * head_dim
    kern = functools.partial(_prompt_attn_kernel, tq=tq, n_heads=n_heads, head_dim=head_dim,
                             n_idx_heads=n_idx_heads, idx_dim=idx_dim, n_sel=n_sel)
    return pl.pallas_call(
        kern,
        grid=(t // tq,),
        in_specs=[
            pl.BlockSpec((n_heads, tq, head_dim), lambda i: (0, i, 0)),
            pl.BlockSpec((tq, n_idx_heads * idx_dim), lambda i: (i, 0)),
            pl.BlockSpec((tq, LANES), lambda i: (i, 0)),
            _resident((t, idx_dim), lambda i: (0, 0)),
            _resident((n_heads, t, head_dim), lambda i: (0, 0, 0)),
            _resident((n_heads, t, head_dim), lambda i: (0, 0, 0)),
        ],
        out_specs=pl.BlockSpec((tq, aw), lambda i: (i, 0)),
        out_shape=jax.ShapeDtypeStruct((t, aw), BF16),
        scratch_shapes=[
            pltpu.VMEM((n_idx_heads, tq, LANES), F32),
            pltpu.VMEM((t // LANES, tq, LANES), I32),
            pltpu.VMEM((t // LANES, tq, LANES), I16),
            pltpu.VMEM((tq, LANES), I32),
            pltpu.VMEM((tq, ATT_CHUNK), F32),
            pltpu.VMEM((n_heads, tq, LANES), F32),
            pltpu.VMEM((n_heads, tq, LANES), F32),
            pltpu.VMEM((n_heads, tq, head_dim), F32),
        ],
        compiler_params=_params("arbitrary"),
        name="prompt_sparse_attention",
    )(q, qi, kiwi, ki, k, v)


def _softplus(x):
    return jnp.maximum(x, 0.0) + jnp.log1p(jnp.exp(-jnp.abs(x)))


def _split_bf16(x):
    hi = x.astype(BF16)
    lo = (x - hi.astype(F32)).astype(BF16)
    return hi, lo


def _lru_gates(xc, wa_ref, ba_ref, wx_ref, bx_ref, lam_ref):
    n_blocks, blk, _ = wa_ref.shape
    ra, rx = [], []
    for n in range(n_blocks):
        x_hi, x_lo = _split_bf16(xc[:, n * blk:(n + 1) * blk])
        for w_ref, dst in ((wa_ref, ra), (wx_ref, rx)):
            w_hi, w_lo = _split_bf16(w_ref[n])
            dst.append(jnp.dot(x_hi, w_hi, preferred_element_type=F32)
                       + jnp.dot(x_hi, w_lo, preferred_element_type=F32)
                       + jnp.dot(x_lo, w_hi, preferred_element_type=F32))
    r = jax.nn.sigmoid(jnp.concatenate(ra, axis=1) + ba_ref[...])
    gate_i = jax.nn.sigmoid(jnp.concatenate(rx, axis=1) + bx_ref[...])
    log_a = (-LRU_C) * r * _softplus(-lam_ref[...])
    a = jnp.exp(log_a)
    b = jnp.sqrt(jnp.tanh(-log_a) * (1.0 + a * a)) * (gate_i * xc)
    return a, b


def _lru_prompt_kernel(x_ref, y_ref, cw_ref, cb_ref, wa_ref, ba_ref, wx_ref, bx_ref, lam_ref,
                       o_ref, hlast_ref, ext_scr, a_scr, b_scr, h_scr, *, tb, conv_width):
    t = pl.program_id(0)
    pad = SUBLANES
    assert conv_width - 1 <= pad

    @pl.when(t == 0)
    def _():
        ext_scr[0:pad, :] = jnp.zeros((pad, ext_scr.shape[1]), F32)
        hlast_ref[...] = jnp.zeros(hlast_ref.shape, F32)

    ext_scr[pad:pad + tb, :] = x_ref[...]
    conv = None
    for j in range(conv_width):
        term = cw_ref[j:j + 1, :] * ext_scr[pl.ds(pad - (conv_width - 1) + j, tb), :]
        conv = term if conv is None else conv + term
    xc = cb_ref[...] + conv
    a, b = _lru_gates(xc, wa_ref, ba_ref, wx_ref, bx_ref, lam_ref)
    a_scr[...] = a
    b_scr[...] = b

    sub = lax.broadcasted_iota(I32, (SUBLANES, 1), 0)

    def tile(j, h_prev):
        r0 = pl.multiple_of(j * SUBLANES, SUBLANES)
        at = a_scr[pl.ds(r0, SUBLANES), :]
        bt = b_scr[pl.ds(r0, SUBLANES), :]
        for d in (1, 2, 4):
            a_sh = pltpu.roll(at, d, 0)
            b_sh = pltpu.roll(bt, d, 0)
            ok = sub >= d
            bt = jnp.where(ok, at * b_sh + bt, bt)
            at = jnp.where(ok, at * a_sh, at)
        h = at * h_prev + bt
        h_scr[pl.ds(r0, SUBLANES), :] = h
        return jnp.broadcast_to(h[SUBLANES - 1:SUBLANES, :], h.shape)

    h_last = lax.fori_loop(0, tb // SUBLANES, tile, hlast_ref[...])
    hlast_ref[...] = h_last
    ext_scr[0:pad, :] = ext_scr[tb:tb + pad, :]
    o_ref[...] = (h_scr[...] * jax.nn.gelu(y_ref[...])).astype(o_ref.dtype)


def _lru_prompt(x_lru, y_lru, conv_w, conv_b, rg_wa, rg_ba, rg_wx, rg_bx, rg_lambda):
    t, w = x_lru.shape
    tb = _row_block(t, 256)
    cw = conv_w.shape[0]
    vec = lambda a: a.reshape(1, w)
    full = lambda shape: pl.BlockSpec(shape, lambda i: (0,) * len(shape))
    out, h_last = pl.pallas_call(
        functools.partial(_lru_prompt_kernel, tb=tb, conv_width=cw),
        grid=(t // tb,),
        in_specs=[pl.BlockSpec((tb, w), lambda i: (i, 0)), pl.BlockSpec((tb, w), lambda i: (i, 0)),
                  full((cw, w)), full((1, w)), full(rg_wa.shape), full((1, w)), full(rg_wx.shape),
                  full((1, w)), full((1, w))],
        out_specs=[pl.BlockSpec((tb, w), lambda i: (i, 0)), full((SUBLANES, w))],
        out_shape=[jax.ShapeDtypeStruct((t, w), BF16), jax.ShapeDtypeStruct((SUBLANES, w), F32)],
        scratch_shapes=[pltpu.VMEM((tb + SUBLANES, w), F32), pltpu.VMEM((tb, w), F32),
                        pltpu.VMEM((tb, w), F32), pltpu.VMEM((tb, w), F32)],
        compiler_params=_params("arbitrary"),
        name="rglru_prompt",
    )(x_lru, y_lru, conv_w, vec(conv_b), rg_wa, vec(rg_ba), rg_wx, vec(rg_bx), vec(rg_lambda))
    return out, h_last[0]


def _lru_step_kernel(x_ref, y_ref, sc_ref, h0_ref, cw_ref, cb_ref, wa_ref, ba_ref, wx_ref, bx_ref,
                     lam_ref, o_ref, h_ref, *, conv_width):
    conv = None
    for j in range(conv_width):
        src = x_ref[...] if j == conv_width - 1 else sc_ref[j]
        term = cw_ref[j:j + 1, :] * src
        conv = term if conv is None else conv + term
    xc = cb_ref[...] + conv
    a, b = _lru_gates(xc, wa_ref, ba_ref, wx_ref, bx_ref, lam_ref)
    h = b + a * h0_ref[...]
    h_ref[...] = h
    o_ref[...] = (h * jax.nn.gelu(y_ref[...])).astype(o_ref.dtype)


def _lru_step(x_lru, y_lru, state_conv_t, h0, conv_w, conv_b, rg_wa, rg_ba, rg_wx, rg_bx, rg_lambda):
    b, w = x_lru.shape
    vec = lambda a: a.reshape(1, w)
    return pl.pallas_call(
        functools.partial(_lru_step_kernel, conv_width=conv_w.shape[0]),
        out_shape=[jax.ShapeDtypeStruct((b, w), BF16), jax.ShapeDtypeStruct((b, w), F32)],
        compiler_params=pltpu.CompilerParams(vmem_limit_bytes=VMEM_LIMIT_BYTES),
        name="rglru_step",
    )(x_lru, y_lru, state_conv_t, h0, conv_w, vec(conv_b), rg_wa, vec(rg_ba), rg_wx, vec(rg_bx),
      vec(rg_lambda))


def _merge_kernel(x_ref, ao_ref, lo_ref, ga_ref, gb_ref, wba_ref, wbl_ref, wout_ref, g2_ref,
                  x1_ref, h2_ref):
    attn = jnp.dot(ao_ref[...], wba_ref[...], preferred_element_type=F32)
    lru = jnp.dot(lo_ref[...], wbl_ref[...], preferred_element_type=F32)
    merged = jax.nn.sigmoid(ga_ref[...]) * attn + jax.nn.sigmoid(gb_ref[...]) * lru
    x1 = x_ref[...] + jnp.dot(merged.astype(BF16), wout_ref[...], preferred_element_type=F32)
    x1_ref[...] = x1
    h2 = x1 * lax.rsqrt(jnp.mean(x1 * x1, axis=-1, keepdims=True) + EPS) * g2_ref[...]
    h2_ref[...] = h2.astype(h2_ref.dtype)


def _merge(x, attn_o, lru_o, ga, gb, w_ba, w_bl, w_out, norm2_g):
    m, d = x.shape
    tm = _row_block(m, 256)
    row = lambda width: pl.BlockSpec((tm, width), lambda i: (i, 0))
    return pl.pallas_call(
        _merge_kernel,
        grid=(m // tm,),
        in_specs=[row(d), row(attn_o.shape[1]), row(lru_o.shape[1]), row(d), row(d),
                  _resident(w_ba.shape, lambda i: (0, 0)), _resident(w_bl.shape, lambda i: (0, 0)),
                  _resident(w_out.shape, lambda i: (0, 0)), pl.BlockSpec((1, d), lambda i: (0, 0))],
        out_specs=[row(d), row(d)],
        out_shape=[jax.ShapeDtypeStruct((m, d), F32), jax.ShapeDtypeStruct((m, d), BF16)],
        compiler_params=_params("parallel"),
        name="gated_merge",
    )(x, attn_o, lru_o, ga, gb, w_ba, w_bl, w_out, norm2_g.reshape(1, d))


def _ffn_kernel(h_ref, x1_ref, wg_ref, wu_ref, wd_ref, gf_ref, y_ref, acc_scr):
    f = pl.program_id(1)

    @pl.when(f == 0)
    def _():
        acc_scr[...] = jnp.zeros(acc_scr.shape, F32)

    h = h_ref[...]
    gate = jnp.dot(h, wg_ref[...], preferred_element_type=F32)
    up = jnp.dot(h, wu_ref[...], preferred_element_type=F32)
    act = (jax.nn.silu(gate) * up).astype(BF16)
    acc_scr[...] += jnp.dot(act, wd_ref[...], preferred_element_type=F32)

    @pl.when(f == pl.num_programs(1) - 1)
    def _():
        x2 = x1_ref[...] + acc_scr[...]
        y_ref[...] = x2 * lax.rsqrt(jnp.mean(x2 * x2, axis=-1, keepdims=True) + EPS) * gf_ref[...]


def _ffn(h2, x1, w_gate, w_up, w_down, norm_f_g):
    m, d = x1.shape
    ff = w_gate.shape[1]
    tm = _row_block(m, 512)
    tf = 512
    assert ff % tf == 0
    return pl.pallas_call(
        _ffn_kernel,
        grid=(m // tm, ff // tf),
        in_specs=[pl.BlockSpec((tm, d), lambda i, f: (i, 0)), pl.BlockSpec((tm, d), lambda i, f: (i, 0)),
                  pl.BlockSpec((d, tf), lambda i, f: (0, f)), pl.BlockSpec((d, tf), lambda i, f: (0, f)),
                  pl.BlockSpec((tf, d), lambda i, f: (f, 0)), pl.BlockSpec((1, d), lambda i, f: (0, 0))],
        out_specs=pl.BlockSpec((tm, d), lambda i, f: (i, 0)),
        out_shape=jax.ShapeDtypeStruct((m, d), F32),
        scratch_shapes=[pltpu.VMEM((tm, d), F32)],
        compiler_params=_params("parallel", "arbitrary"),
        name="swiglu_ffn",
    )(h2, x1, w_gate, w_up, w_down, norm_f_g.reshape(1, d))


SCORE_PAGES = 16


def _sample_score_kernel(pt_ref, qi_ref, wi_ref, kin_ref, kidx_hbm, keys_ref, keynew_ref,
                         kibuf, sems, *, n_pages, page, n_idx_heads, idx_dim):
    b = pl.program_id(0)
    slot = b % 2
    idx_scale = (idx_dim ** -0.5) * (n_idx_heads ** -0.5)

    def page_copy(seq, sl, p):
        return pltpu.make_async_copy(kidx_hbm.at[pt_ref[seq, p]], kibuf.at[sl, p], sems.at[sl])

    def start_all(seq, sl):
        def body(p, c):
            page_copy(seq, sl, p).start()
            return c

        lax.fori_loop(0, n_pages, body, 0)

    @pl.when(b == 0)
    def _():
        start_all(0, 0)

    @pl.when(b + 1 < pl.num_programs(0))
    def _():
        start_all(b + 1, 1 - slot)

    def wait(p, c):
        page_copy(b, slot, p).wait()
        return c

    lax.fori_loop(0, n_pages, wait, 0)

    qi = qi_ref[0]
    wcol = wi_ref[0] * idx_scale
    wb = jnp.broadcast_to(wcol, (n_idx_heads, page))

    def score(g, c):
        p0 = pl.multiple_of(g * SCORE_PAGES, SCORE_PAGES)
        pages = kibuf[slot, pl.ds(p0, SCORE_PAGES)]
        kblk = jnp.concatenate([pages[j] for j in range(SCORE_PAGES)], axis=1).astype(BF16)
        s = jnp.dot(qi, kblk, preferred_element_type=F32)
        rows = []
        for j in range(SCORE_PAGES):
            sj = jnp.maximum(s[:, j * page:(j + 1) * page], 0.0) * wb
            rows.append(jnp.sum(sj, axis=0, keepdims=True))
        keys_ref[0, pl.ds(p0, SCORE_PAGES), :] = _sortable_key(jnp.concatenate(rows, axis=0))
        return c

    lax.fori_loop(0, n_pages // SCORE_PAGES, score, 0)

    s_new = jnp.sum(qi.astype(F32) * kin_ref[0].astype(BF16).astype(F32), axis=1, keepdims=True)
    i_new = jnp.sum(jnp.maximum(s_new, 0.0) * wcol, axis=0, keepdims=True)
    keynew_ref[0] = jnp.broadcast_to(_sortable_key(i_new), (1, LANES))


def _sample_pick_kernel(keys_ref, keynew_ref, pt_ref, rows_ref, sel_scr, selnew_scr,
                        *, n_pages, page, n_sel):
    bsz = keys_ref.shape[0]
    key_new = keynew_ref[...][:, :, :1]

    def count_ge(cand):
        cnt = jnp.sum(jnp.where(keys_ref[...] >= cand, 1.0, 0.0), axis=1, keepdims=True)
        cnt = jnp.sum(cnt, axis=2, keepdims=True)
        return cnt + jnp.where(key_new >= cand, 1.0, 0.0)

    thr = _kth_largest_key(count_ge, n_sel, (bsz, 1, 1))
    sel_scr[...] = jnp.where(keys_ref[...] >= thr, 1.0, 0.0)
    selnew_scr[...] = jnp.broadcast_to(jnp.where(key_new >= thr, 1, 0), selnew_scr.shape)

    tri_o = jnp.where(lax.broadcasted_iota(I32, (page, page), 1) <= lax.broadcasted_iota(I32, (page, page), 0),
                      1.0, 0.0).astype(BF16)
    tri_p = jnp.where(lax.broadcasted_iota(I32, (n_pages, n_pages), 1)
                      <= lax.broadcasted_iota(I32, (n_pages, n_pages), 0), 1.0, 0.0).astype(BF16)
    j_row = lax.broadcasted_iota(I32, (1, n_sel), 1).astype(F32)
    p_col = lax.broadcasted_iota(I32, (n_pages, 1), 0).astype(F32)
    reps = n_sel // LANES

    def invert(b, c):
        sel = sel_scr[b]
        cnt = jnp.sum(sel, axis=1, keepdims=True)
        cnt_b = jnp.broadcast_to(cnt, (n_pages, LANES))
        cum_b = jnp.dot(tri_p, cnt_b.astype(BF16), preferred_element_type=F32)
        base_b = cum_b - cnt_b
        tile = lambda a: jnp.concatenate([a] * reps, axis=1)
        page_of = jnp.sum(jnp.where(tile(cum_b) <= j_row, 1.0, 0.0), axis=0, keepdims=True)
        onehot = jnp.where(p_col == page_of, 1.0, 0.0)
        base_j = jnp.sum(onehot * tile(base_b), axis=0, keepdims=True)
        pt_col = pt_ref[b].astype(F32)
        phys_j = jnp.sum(onehot * pt_col, axis=0, keepdims=True)
        incl_t = lax.dot_general(tri_o, sel.astype(BF16), _NT_DIMS, preferred_element_type=F32)
        incl_j = jnp.dot(incl_t.astype(BF16), onehot.astype(BF16), preferred_element_type=F32)
        off_j = jnp.sum(jnp.where(incl_j <= j_row - base_j, 1.0, 0.0), axis=0, keepdims=True)
        row = (phys_j * page + off_j).astype(I32)
        total = jnp.sum(cnt, axis=0, keepdims=True)
        is_new = (selnew_scr[b][:1, :1] > 0) & (j_row == n_sel - 1)
        rows_ref[pl.ds(b, 1), :] = jnp.where(is_new | (j_row >= total), -1, row)
        return c

    lax.fori_loop(0, bsz, invert, 0)


def _sample_select(qi, wi, ki_new, cache_kidx, page_table, *, n_sel):
    bsz, n_idx_heads, idx_dim = qi.shape
    n_pages = page_table.shape[1]
    n_pool, page, _ = cache_kidx.shape
    assert page == LANES and n_pages % SCORE_PAGES == 0 and n_sel % LANES == 0
    assert n_pool * page < 2 ** 24
    kidx_t = jnp.swapaxes(cache_kidx, 1, 2)
    keys, key_new = pl.pallas_call(
        functools.partial(_sample_score_kernel, n_pages=n_pages, page=page, n_idx_heads=n_idx_heads,
                          idx_dim=idx_dim),
        grid_spec=pltpu.PrefetchScalarGridSpec(
            num_scalar_prefetch=1,
            grid=(bsz,),
            in_specs=[pl.BlockSpec((1, n_idx_heads, idx_dim), lambda b, pt: (b, 0, 0)),
                      pl.BlockSpec((1, n_idx_heads, 1), lambda b, pt: (b, 0, 0)),
                      pl.BlockSpec((1, 1, idx_dim), lambda b, pt: (b, 0, 0)),
                      pl.BlockSpec(memory_space=pl.ANY)],
            out_specs=[pl.BlockSpec((1, n_pages, page), lambda b, pt: (b, 0, 0)),
                       pl.BlockSpec((1, 1, LANES), lambda b, pt: (b, 0, 0))],
            scratch_shapes=[pltpu.VMEM((2, n_pages, idx_dim, page), F32), pltpu.SemaphoreType.DMA((2,))],
        ),
        out_shape=[jax.ShapeDtypeStruct((bsz, n_pages, page), I32),
                   jax.ShapeDtypeStruct((bsz, 1, LANES), I32)],
        compiler_params=_params("arbitrary"),
        name="sample_index_scores",
    )(page_table, qi, wi, ki_new, kidx_t)
    return pl.pallas_call(
        functools.partial(_sample_pick_kernel, n_pages=n_pages, page=page, n_sel=n_sel),
        out_shape=jax.ShapeDtypeStruct((bsz, n_sel), I32),
        scratch_shapes=[pltpu.VMEM((bsz, n_pages, page), F32), pltpu.VMEM((bsz, SUBLANES, LANES), I32)],
        compiler_params=pltpu.CompilerParams(vmem_limit_bytes=VMEM_LIMIT_BYTES),
        name="sample_index_pick",
    )(keys, key_new, page_table[:, :, None])


def _sample_attend_kernel(rows_sm, rowsv_ref, q_ref, kn_ref, vn_ref, ck_hbm, cv_hbm, o_ref,
                          kbuf, vbuf, sems, *, n_sel, n_heads, head_dim):
    b = pl.program_id(0)
    slot = b % 2

    def copies(seq, sl, j):
        row = jnp.maximum(rows_sm[seq, j], 0)
        return (pltpu.make_async_copy(ck_hbm.at[row], kbuf.at[sl, j], sems.at[0, sl]),
                pltpu.make_async_copy(cv_hbm.at[row], vbuf.at[sl, j], sems.at[1, sl]))

    def start_all(seq, sl):
        def body(j, c):
            ck, cv = copies(seq, sl, j)
            ck.start()
            cv.start()
            return c

        lax.fori_loop(0, n_sel, body, 0)

    @pl.when(b == 0)
    def _():
        start_all(0, 0)

    @pl.when(b + 1 < pl.num_programs(0))
    def _():
        start_all(b + 1, 1 - slot)

    def wait(j, c):
        ck, cv = copies(b, slot, j)
        ck.wait()
        cv.wait()
        return c

    lax.fori_loop(0, n_sel, wait, 0)

    from_new = rowsv_ref[0] < 0
    for h in range(n_heads):
        k_h = jnp.where(from_new, kn_ref[0, h:h + 1, :], kbuf[slot, :, h, :])
        v_h = jnp.where(from_new, vn_ref[0, h:h + 1, :], vbuf[slot, :, h, :])
        q_h = q_ref[0, h:h + 1, :]
        s = lax.dot_general(q_h, k_h.astype(BF16), _NT_DIMS, preferred_element_type=F32)
        m = jnp.max(s, axis=1, keepdims=True)
        p = jnp.exp2(s - m)
        o = jnp.dot(p.astype(BF16), v_h.astype(BF16), preferred_element_type=F32)
        o_ref[0, h:h + 1, :] = (o / jnp.sum(p, axis=1, keepdims=True)).astype(o_ref.dtype)


def _sample_attend(rows, q, k_new, v_new, cache_k, cache_v):
    bsz, n_sel = rows.shape
    n_pool, page, n_heads, head_dim = cache_k.shape
    kern = functools.partial(_sample_attend_kernel, n_sel=n_sel, n_heads=n_heads, head_dim=head_dim)
    per_seq = lambda: pl.BlockSpec((1, n_heads, head_dim), lambda b, r_sm: (b, 0, 0))
    flat = lambda c: c.reshape(n_pool * page, n_heads, head_dim)
    return pl.pallas_call(
        kern,
        grid_spec=pltpu.PrefetchScalarGridSpec(
            num_scalar_prefetch=1,
            grid=(bsz,),
            in_specs=[pl.BlockSpec((1, n_sel, 1), lambda b, r_sm: (b, 0, 0)),
                      per_seq(), per_seq(), per_seq(),
                      pl.BlockSpec(memory_space=pl.ANY), pl.BlockSpec(memory_space=pl.ANY)],
            out_specs=per_seq(),
            scratch_shapes=[pltpu.VMEM((2, n_sel, n_heads, head_dim), F32),
                            pltpu.VMEM((2, n_sel, n_heads, head_dim), F32),
                            pltpu.SemaphoreType.DMA((2, 2))],
        ),
        out_shape=jax.ShapeDtypeStruct((bsz, n_heads, head_dim), BF16),
        compiler_params=_params("arbitrary"),
        name="sample_sparse_attention",
    )(rows, rows[:, :, None], q, k_new, v_new, flat(cache_k), flat(cache_v))


def kernel(x_prompt, x_sample, cache_k, cache_v, cache_kidx, state_conv, state_rglru, page_table,
           norm1_g, w_in, conv_w, conv_b, rg_wa, rg_ba, rg_wx, rg_bx, rg_lambda,
           w_branch_attn, w_branch_lru, w_out, norm2_g, w_ffn_gate, w_ffn_up, w_ffn_down, norm_f_g):
    bp, seq, d_model = x_prompt.shape
    bd, dec_seq, _ = x_sample.shape
    _, page, n_heads, head_dim = cache_k.shape
    idx_dim = cache_kidx.shape[2]
    lru_w = conv_w.shape[1]
    conv_width = conv_w.shape[0]
    attn_w = n_heads * head_dim
    n_idx_heads = (w_in.shape[1] - 3 * attn_w - idx_dim - 2 * lru_w - 2 * d_model) // (idx_dim + 1)
    assert bp == 1 and dec_seq == 1
    assert idx_dim + n_idx_heads <= LANES

    sizes = [attn_w, attn_w, attn_w, n_idx_heads * idx_dim, idx_dim + n_idx_heads, lru_w, lru_w,
             d_model, d_model]
    cuts = np.cumsum([0] + sizes)
    w_q, w_k, w_v, w_qi, w_kiwi, w_xl, w_yl, w_ga, w_gb = (
        w_in[:, cuts[n]:cuts[n + 1]].astype(BF16) for n in range(len(sizes)))
    w_kiwi = jnp.pad(w_kiwi, ((0, 0), (0, LANES - w_kiwi.shape[1])))
    tail_w = (w_branch_attn.astype(BF16), w_branch_lru.astype(BF16), w_out.astype(BF16), norm2_g)
    ffn_w = (w_ffn_gate.astype(BF16), w_ffn_up.astype(BF16), w_ffn_down.astype(BF16), norm_f_g)
    lru_p = (conv_w, conv_b, rg_wa, rg_ba, rg_wx, rg_bx, rg_lambda)

    def project(x2d, by_head):
        hd = head_dim if by_head else None
        h = _rmsnorm_bf16(x2d, norm1_g)
        (q,) = _matmul(h, w_q, [BF16], out_scale=head_dim ** -0.5 * LOG2_E, head_dim=hd)
        k, k_bf = _matmul(h, w_k, [F32, BF16], head_dim=hd)
        v, v_bf = _matmul(h, w_v, [F32, BF16], head_dim=hd)
        (qi,) = _matmul(h, w_qi, [BF16])
        (kiwi,) = _matmul(h, w_kiwi, [F32])
        (x_lru,) = _matmul(h, w_xl, [F32])
        (y_lru,) = _matmul(h, w_yl, [F32])
        (ga,) = _matmul(h, w_ga, [F32])
        (gb,) = _matmul(h, w_gb, [F32])
        return q, k, k_bf, v, v_bf, qi, kiwi, x_lru, y_lru, ga, gb

    def finish(x2d, attn_o, lru_o, ga, gb):
        x1, h2 = _merge(x2d, attn_o, lru_o, ga, gb, *tail_w)
        return _ffn(h2, x1, *ffn_w)

    xp = x_prompt.reshape(seq, d_model)
    q, k_p, k_bf, v_p, v_bf, qi, kiwi, x_lru, y_lru, ga, gb = project(xp, by_head=True)
    ki_p = kiwi[:, :idx_dim]
    attn_p = _prompt_attention(q, qi, kiwi, ki_p.astype(BF16), k_bf, v_bf, n_heads=n_heads,
                               head_dim=head_dim, n_idx_heads=n_idx_heads, idx_dim=idx_dim)
    lru_o, h_p = _lru_prompt(x_lru, y_lru, *lru_p)
    y_prompt = finish(xp, attn_p, lru_o, ga, gb).reshape(bp, seq, d_model)
    conv_p = x_lru[seq - (conv_width - 1):].reshape(bp, conv_width - 1, lru_w)

    xs = x_sample.reshape(bd, d_model)
    q, k_s, _, v_s, _, qi, kiwi, x_lru, y_lru, ga, gb = project(xs, by_head=False)
    ki_s = kiwi[:, :idx_dim]
    wi_s = kiwi[:, idx_dim:idx_dim + n_idx_heads]
    past = page_table.shape[1] * page
    n_sel = min(TOPK_MAX, (past + dec_seq) // 4)
    rows = _sample_select(qi.reshape(bd, n_idx_heads, idx_dim), wi_s[:, :, None], ki_s[:, None, :],
                          cache_kidx, page_table, n_sel=n_sel)
    hd = (bd, n_heads, head_dim)
    attn_s = _sample_attend(rows, q.reshape(hd), k_s.reshape(hd), v_s.reshape(hd), cache_k, cache_v)
    lru_o, h_s = _lru_step(x_lru, y_lru, jnp.swapaxes(state_conv, 0, 1), state_rglru, *lru_p)
    y_sample = finish(xs, attn_s.reshape(bd, attn_w), lru_o, ga, gb).reshape(bd, dec_seq, d_model)
    conv_s = jnp.concatenate([state_conv[:, 1:], x_lru[:, None, :]], axis=1)

    return (y_prompt, y_sample,
            k_p.reshape(bp, seq, n_heads, head_dim), v_p.reshape(bp, seq, n_heads, head_dim),
            ki_p.reshape(bp, seq, idx_dim), conv_p, h_p.reshape(bp, lru_w),
            k_s.reshape(bd, dec_seq, n_heads, head_dim), v_s.reshape(bd, dec_seq, n_heads, head_dim),
            ki_s.reshape(bd, dec_seq, idx_dim), conv_s, h_s)
```

```python
import functools

import jax
import jax.numpy as jnp
import numpy as np
from jax import lax
from jax.experimental import pallas as pl
from jax.experimental.pallas import tpu as pltpu

TOPK_MAX = 256
LRU_C = 8.0
EPS = 1e-6
MASK_VALUE = -1e30
LOG2_E = 1.4426950408889634

V7X_VMEM_BYTES = 64 * 1024 * 1024
VMEM_LIMIT_BYTES = 60 * 1024 * 1024
LANES = 128
SUBLANES = 8

BF16 = jnp.bfloat16
F32 = jnp.float32
I32 = jnp.int32
I16 = jnp.int16
INT16_MIN = -(2 ** 15)
INT16_MAX = 2 ** 15 - 1

_NT_DIMS = (((1,), (1,)), ((), ()))


def _params(*sem):
    return pltpu.CompilerParams(dimension_semantics=sem, vmem_limit_bytes=VMEM_LIMIT_BYTES)


def _resident(block_shape, index_map):
    return pl.BlockSpec(block_shape, index_map, pipeline_mode=pl.Buffered(1))


def _row_block(m, target):
    tm = min(m, target)
    assert m % tm == 0, (m, tm)
    return tm


def _rmsnorm_kernel(x_ref, g_ref, o_ref):
    x = x_ref[...]
    y = x * lax.rsqrt(jnp.mean(x * x, axis=-1, keepdims=True) + EPS) * g_ref[...]
    o_ref[...] = y.astype(o_ref.dtype)


def _rmsnorm_bf16(x, g):
    m, d = x.shape
    tm = _row_block(m, 512)
    return pl.pallas_call(
        _rmsnorm_kernel,
        grid=(m // tm,),
        in_specs=[pl.BlockSpec((tm, d), lambda i: (i, 0)), pl.BlockSpec((1, d), lambda i: (0, 0))],
        out_specs=pl.BlockSpec((tm, d), lambda i: (i, 0)),
        out_shape=jax.ShapeDtypeStruct((m, d), BF16),
        compiler_params=_params("parallel"),
        name="rmsnorm_bf16",
    )(x, g.reshape(1, d))


def _matmul_kernel(a_ref, w_ref, *o_refs, out_scale):
    acc = jnp.dot(a_ref[...], w_ref[...], preferred_element_type=F32)
    if out_scale is not None:
        acc = acc * out_scale
    for o_ref in o_refs:
        if len(o_ref.shape) == 3:
            hd = o_ref.shape[2]
            for h in range(o_ref.shape[0]):
                o_ref[h] = acc[:, h * hd:(h + 1) * hd].astype(o_ref.dtype)
        else:
            o_ref[...] = acc.astype(o_ref.dtype)


def _matmul(a, w, out_dtypes, out_scale=None, head_dim=None):
    m, k = a.shape
    n = w.shape[1]
    tm = _row_block(m, 1024)
    tn = _row_block(n, 1024)
    by_head = [head_dim is not None and dt == BF16 for dt in out_dtypes]
    assert head_dim is None or tn % head_dim == 0
    outs = pl.pallas_call(
        functools.partial(_matmul_kernel, out_scale=out_scale),
        grid=(n // tn, m // tm),
        in_specs=[pl.BlockSpec((tm, k), lambda j, i: (i, 0)), pl.BlockSpec((k, tn), lambda j, i: (0, j))],
        out_specs=[pl.BlockSpec((tn // head_dim, tm, head_dim), lambda j, i: (j, i, 0)) if hm
                   else pl.BlockSpec((tm, tn), lambda j, i: (i, j)) for hm in by_head],
        out_shape=[jax.ShapeDtypeStruct((n // head_dim, m, head_dim) if hm else (m, n), dt)
                   for hm, dt in zip(by_head, out_dtypes)],
        compiler_params=_params("parallel", "parallel"),
        name="proj_matmul",
    )(a, w)
    return outs


def _sortable_key(x):
    x = jnp.where(x == 0.0, 0.0, x)
    bits = pltpu.bitcast(x, I32)
    return bits ^ ((bits >> 31) & jnp.int32(0x7FFFFFFF))


def _kth_largest_key(count_ge, k, shape):
    int_min = jnp.int32(-(2**31))
    zero = jnp.zeros(shape, I32)
    ans = jnp.where(count_ge(zero) >= k, zero, jnp.full(shape, int_min, I32))

    def body(it, ans):
        cand = ans | (jnp.int32(1) << (jnp.int32(30) - it))
        return jnp.where(count_ge(cand) >= k, cand, ans)

    return lax.fori_loop(0, 31, body, ans)


def _kth_largest_half(count_ge, k, shape):
    def body(it, ans):
        cand = ans + (jnp.int32(1) << (jnp.int32(15) - it))
        return jnp.where(count_ge(cand) >= k, cand, ans)

    return lax.fori_loop(0, 16, body, jnp.full(shape, INT16_MIN, I32))


IDX_SUB = 256
KV_CHUNK = 512
ATT_CHUNK = 256


def _prompt_attn_kernel(q_ref, qi_ref, kiwi_ref, ki_ref, k_ref, v_ref, o_ref,
                        wb_scr, keys_scr, half_scr, thr_scr, bias_scr, m_scr, l_scr, acc_scr,
                        *, tq, n_heads, head_dim, n_idx_heads, idx_dim, n_sel, max_chunks):
    i = pl.program_id(0)
    row0 = i * tq
    n_chunks = (row0 + tq + KV_CHUNK - 1) // KV_CHUNK
    idx_scale = (idx_dim ** -0.5) * (n_idx_heads ** -0.5)
    rows = row0 + lax.broadcasted_iota(I32, (tq, 1), 0)

    wi = kiwi_ref[:, idx_dim:idx_dim + n_idx_heads] * idx_scale
    for h in range(n_idx_heads):
        wb_scr[h] = jnp.broadcast_to(wi[:, h:h + 1], (tq, LANES))

    lane_reps = KV_CHUNK // LANES

    def score_chunk(c, carry):
        for sub in range(KV_CHUNK // IDX_SUB):
            col0 = pl.multiple_of(c * KV_CHUNK + sub * IDX_SUB, IDX_SUB)
            ki_c = ki_ref[pl.ds(col0, IDX_SUB), :]
            acc = jnp.zeros((tq, IDX_SUB), F32)
            for h in range(n_idx_heads):
                s = lax.dot_general(qi_ref[:, h * idx_dim:(h + 1) * idx_dim], ki_c, _NT_DIMS,
                                    preferred_element_type=F32)
                wb = wb_scr[h]
                acc = acc + jnp.maximum(s, 0.0) * jnp.concatenate([wb] * (IDX_SUB // LANES), axis=1)
            cols = col0 + lax.broadcasted_iota(I32, (1, IDX_SUB), 1)
            acc = jnp.where(cols <= rows, acc, -jnp.inf)
            key = _sortable_key(acc)
            blk0 = c * lane_reps + sub * (IDX_SUB // LANES)
            for j in range(IDX_SUB // LANES):
                key_j = key[:, j * LANES:(j + 1) * LANES]
                keys_scr[blk0 + j] = key_j
                half_scr[blk0 + j] = (key_j >> 16).astype(I16)
        return carry

    lax.fori_loop(0, n_chunks, score_chunk, 0)

    def count_ge16(cand):
        cand_b = jnp.broadcast_to(cand.astype(I16), (tq, LANES))

        def counter(n):
            def run():
                acc = jnp.zeros((tq, LANES), I16)
                for blk in range(n * lane_reps):
                    acc = acc + jnp.where(half_scr[blk] >= cand_b, jnp.int16(1), jnp.int16(0))
                return acc
            return run

        acc = lax.switch(n_chunks - 1, [counter(n) for n in range(1, max_chunks + 1)])
        return jnp.sum(acc.astype(F32), axis=1, keepdims=True).astype(I32)

    def search_half(stage, carry):
        thr_hi, _ = carry

        @pl.when(stage == 1)
        def _():
            def keep_low_halves(c, carry):
                for j in range(lane_reps):
                    key = keys_scr[c * lane_reps + j]
                    high = key >> 16
                    low = (key & 0xFFFF) + INT16_MIN
                    low = jnp.where(high == thr_hi, low, jnp.where(high > thr_hi, INT16_MAX, INT16_MIN))
                    half_scr[c * lane_reps + j] = low.astype(I16)
                return carry

            lax.fori_loop(0, n_chunks, keep_low_halves, 0)

        found = _kth_largest_half(count_ge16, n_sel, (tq, 1))
        return jnp.where(stage == 0, found, thr_hi), found

    zeros = jnp.zeros((tq, 1), I32)
    thr_hi, thr_lo = lax.fori_loop(0, 2, search_half, (zeros, zeros))
    thr = (thr_hi << 16) | (thr_lo - INT16_MIN)
    thr_scr[...] = jnp.broadcast_to(thr, (tq, LANES))

    m_scr[...] = jnp.full(m_scr.shape, MASK_VALUE, F32)
    l_scr[...] = jnp.zeros(l_scr.shape, F32)
    acc_scr[...] = jnp.zeros(acc_scr.shape, F32)

    att_reps = ATT_CHUNK // LANES

    def attend_chunk(c, carry):
        col0 = pl.multiple_of(c * ATT_CHUNK, ATT_CHUNK)
        cols = col0 + lax.broadcasted_iota(I32, (1, ATT_CHUNK), 1)
        thr_b = jnp.concatenate([thr_scr[...]] * att_reps, axis=1)
        keys = jnp.concatenate([keys_scr[c * att_reps + j] for j in range(att_reps)], axis=1)
        sel = (keys >= thr_b) & (cols <= rows)
        bias_scr[...] = jnp.where(sel, 0.0, MASK_VALUE)
        for h in range(n_heads):
            s = lax.dot_general(q_ref[h], k_ref[h, pl.ds(col0, ATT_CHUNK), :], _NT_DIMS,
                                preferred_element_type=F32) + bias_scr[...]
            m_prev = m_scr[h]
            m_new = jnp.maximum(m_prev, jnp.max(s, axis=1, keepdims=True))
            alpha = jnp.exp2(m_prev - m_new)
            p = jnp.exp2(s - jnp.concatenate([m_new] * att_reps, axis=1))
            l_scr[h] = alpha * l_scr[h] + jnp.sum(p, axis=1, keepdims=True)
            acc_scr[h] = alpha * acc_scr[h] + jnp.dot(p.astype(BF16), v_ref[h, pl.ds(col0, ATT_CHUNK), :],
                                                      preferred_element_type=F32)
            m_scr[h] = m_new
        return carry

    lax.fori_loop(0, (row0 + tq + ATT_CHUNK - 1) // ATT_CHUNK, attend_chunk, 0)

    for h in range(n_heads):
        o_ref[:, h * head_dim:(h + 1) * head_dim] = (acc_scr[h] / l_scr[h]).astype(o_ref.dtype)


def _prompt_attention(q, qi, kiwi, ki, k, v, *, n_heads, head_dim, n_idx_heads, idx_dim):
    t = q.shape[1]
    tq = _row_block(t, 256)
    assert t % KV_CHUNK == 0 and t >= TOPK_MAX * 2 and head_dim == LANES
    n_sel = min(TOPK_MAX, t // 4)
    aw = n_heads * head_dim
    kern = functools.partial(_prompt_attn_kernel, tq=tq, n_heads=n_heads, head_dim=head_dim,
                             n_idx_heads=n_idx_heads, idx_dim=idx_dim, n_sel=n_sel,
                             max_chunks=t // KV_CHUNK)
    return pl.pallas_call(
        kern,
        grid=(t // tq,),
        in_specs=[
            pl.BlockSpec((n_heads, tq, head_dim), lambda i: (0, i, 0)),
            pl.BlockSpec((tq, n_idx_heads * idx_dim), lambda i: (i, 0)),
            pl.BlockSpec((tq, LANES), lambda i: (i, 0)),
            _resident((t, idx_dim), lambda i: (0, 0)),
            _resident((n_heads, t, head_dim), lambda i: (0, 0, 0)),
            _resident((n_heads, t, head_dim), lambda i: (0, 0, 0)),
        ],
        out_specs=pl.BlockSpec((tq, aw), lambda i: (i, 0)),
        out_shape=jax.ShapeDtypeStruct((t, aw), BF16),
        scratch_shapes=[
            pltpu.VMEM((n_idx_heads, tq, LANES), F32),
            pltpu.VMEM((t // LANES, tq, LANES), I32),
            pltpu.VMEM((t // LANES, tq, LANES), I16),
            pltpu.VMEM((tq, LANES), I32),
            pltpu.VMEM((tq, ATT_CHUNK), F32),
            pltpu.VMEM((n_heads, tq, LANES), F32),
            pltpu.VMEM((n_heads, tq, LANES), F32),
            pltpu.VMEM((n_heads, tq, head_dim), F32),
        ],
        compiler_params=_params("arbitrary"),
        name="prompt_sparse_attention",
    )(q, qi, kiwi, ki, k, v)


def _softplus(x):
    return jnp.maximum(x, 0.0) + jnp.log1p(jnp.exp(-jnp.abs(x)))


def _split_bf16(x):
    hi = x.astype(BF16)
    lo = (x - hi.astype(F32)).astype(BF16)
    return hi, lo


def _lru_gates(xc, wa_ref, ba_ref, wx_ref, bx_ref, lam_ref):
    n_blocks, blk, _ = wa_ref.shape
    ra, rx = [], []
    for n in range(n_blocks):
        x_hi, x_lo = _split_bf16(xc[:, n * blk:(n + 1) * blk])
        for w_ref, dst in ((wa_ref, ra), (wx_ref, rx)):
            w_hi, w_lo = _split_bf16(w_ref[n])
            dst.append(jnp.dot(x_hi, w_hi, preferred_element_type=F32)
                       + jnp.dot(x_hi, w_lo, preferred_element_type=F32)
                       + jnp.dot(x_lo, w_hi, preferred_element_type=F32))
    r = jax.nn.sigmoid(jnp.concatenate(ra, axis=1) + ba_ref[...])
    gate_i = jax.nn.sigmoid(jnp.concatenate(rx, axis=1) + bx_ref[...])
    log_a = (-LRU_C) * r * _softplus(-lam_ref[...])
    a = jnp.exp(log_a)
    b = jnp.sqrt(jnp.tanh(-log_a) * (1.0 + a * a)) * (gate_i * xc)
    return a, b


SCAN_UNROLL = 8
DMA_UNROLL = 8


def _lru_prompt_kernel(x_ref, y_ref, cw_ref, cb_ref, wa_ref, ba_ref, wx_ref, bx_ref, lam_ref,
                       o_ref, hlast_ref, ext_scr, a_scr, b_scr, h_scr, *, tb, conv_width):
    t = pl.program_id(0)
    pad = SUBLANES
    assert conv_width - 1 <= pad

    @pl.when(t == 0)
    def _():
        ext_scr[0:pad, :] = jnp.zeros((pad, ext_scr.shape[1]), F32)
        hlast_ref[...] = jnp.zeros(hlast_ref.shape, F32)

    ext_scr[pad:pad + tb, :] = x_ref[...]
    conv = None
    for j in range(conv_width):
        term = cw_ref[j:j + 1, :] * ext_scr[pl.ds(pad - (conv_width - 1) + j, tb), :]
        conv = term if conv is None else conv + term
    xc = cb_ref[...] + conv
    a, b = _lru_gates(xc, wa_ref, ba_ref, wx_ref, bx_ref, lam_ref)
    a_scr[...] = a
    b_scr[...] = b

    sub = lax.broadcasted_iota(I32, (SUBLANES, 1), 0)

    def tile(j, h_prev):
        r0 = pl.multiple_of(j * SUBLANES, SUBLANES)
        at = a_scr[pl.ds(r0, SUBLANES), :]
        bt = b_scr[pl.ds(r0, SUBLANES), :]
        for d in (1, 2, 4):
            a_sh = pltpu.roll(at, d, 0)
            b_sh = pltpu.roll(bt, d, 0)
            ok = sub >= d
            bt = jnp.where(ok, at * b_sh + bt, bt)
            at = jnp.where(ok, at * a_sh, at)
        h = at * h_prev + bt
        h_scr[pl.ds(r0, SUBLANES), :] = h
        return jnp.broadcast_to(h[SUBLANES - 1:SUBLANES, :], h.shape)

    h_last = lax.fori_loop(0, tb // SUBLANES, tile, hlast_ref[...], unroll=SCAN_UNROLL)
    hlast_ref[...] = h_last
    ext_scr[0:pad, :] = ext_scr[tb:tb + pad, :]
    o_ref[...] = (h_scr[...] * jax.nn.gelu(y_ref[...])).astype(o_ref.dtype)


def _lru_prompt(x_lru, y_lru, conv_w, conv_b, rg_wa, rg_ba, rg_wx, rg_bx, rg_lambda):
    t, w = x_lru.shape
    tb = _row_block(t, 256)
    cw = conv_w.shape[0]
    vec = lambda a: a.reshape(1, w)
    full = lambda shape: pl.BlockSpec(shape, lambda i: (0,) * len(shape))
    out, h_last = pl.pallas_call(
        functools.partial(_lru_prompt_kernel, tb=tb, conv_width=cw),
        grid=(t // tb,),
        in_specs=[pl.BlockSpec((tb, w), lambda i: (i, 0)), pl.BlockSpec((tb, w), lambda i: (i, 0)),
                  full((cw, w)), full((1, w)), full(rg_wa.shape), full((1, w)), full(rg_wx.shape),
                  full((1, w)), full((1, w))],
        out_specs=[pl.BlockSpec((tb, w), lambda i: (i, 0)), full((SUBLANES, w))],
        out_shape=[jax.ShapeDtypeStruct((t, w), BF16), jax.ShapeDtypeStruct((SUBLANES, w), F32)],
        scratch_shapes=[pltpu.VMEM((tb + SUBLANES, w), F32), pltpu.VMEM((tb, w), F32),
                        pltpu.VMEM((tb, w), F32), pltpu.VMEM((tb, w), F32)],
        compiler_params=_params("arbitrary"),
        name="rglru_prompt",
    )(x_lru, y_lru, conv_w, vec(conv_b), rg_wa, vec(rg_ba), rg_wx, vec(rg_bx), vec(rg_lambda))
    return out, h_last[0]


def _lru_step_kernel(x_ref, y_ref, sc_ref, h0_ref, cw_ref, cb_ref, wa_ref, ba_ref, wx_ref, bx_ref,
                     lam_ref, o_ref, h_ref, *, conv_width):
    conv = None
    for j in range(conv_width):
        src = x_ref[...] if j == conv_width - 1 else sc_ref[j]
        term = cw_ref[j:j + 1, :] * src
        conv = term if conv is None else conv + term
    xc = cb_ref[...] + conv
    a, b = _lru_gates(xc, wa_ref, ba_ref, wx_ref, bx_ref, lam_ref)
    h = b + a * h0_ref[...]
    h_ref[...] = h
    o_ref[...] = (h * jax.nn.gelu(y_ref[...])).astype(o_ref.dtype)


def _lru_step(x_lru, y_lru, state_conv_t, h0, conv_w, conv_b, rg_wa, rg_ba, rg_wx, rg_bx, rg_lambda):
    b, w = x_lru.shape
    vec = lambda a: a.reshape(1, w)
    return pl.pallas_call(
        functools.partial(_lru_step_kernel, conv_width=conv_w.shape[0]),
        out_shape=[jax.ShapeDtypeStruct((b, w), BF16), jax.ShapeDtypeStruct((b, w), F32)],
        compiler_params=pltpu.CompilerParams(vmem_limit_bytes=VMEM_LIMIT_BYTES),
        name="rglru_step",
    )(x_lru, y_lru, state_conv_t, h0, conv_w, vec(conv_b), rg_wa, vec(rg_ba), rg_wx, vec(rg_bx),
      vec(rg_lambda))


def _merge_kernel(x_ref, ao_ref, lo_ref, ga_ref, gb_ref, wba_ref, wbl_ref, wout_ref, g2_ref,
                  x1_ref, h2_ref):
    attn = jnp.dot(ao_ref[...], wba_ref[...], preferred_element_type=F32)
    lru = jnp.dot(lo_ref[...], wbl_ref[...], preferred_element_type=F32)
    merged = jax.nn.sigmoid(ga_ref[...]) * attn + jax.nn.sigmoid(gb_ref[...]) * lru
    x1 = x_ref[...] + jnp.dot(merged.astype(BF16), wout_ref[...], preferred_element_type=F32)
    x1_ref[...] = x1
    h2 = x1 * lax.rsqrt(jnp.mean(x1 * x1, axis=-1, keepdims=True) + EPS) * g2_ref[...]
    h2_ref[...] = h2.astype(h2_ref.dtype)


def _merge(x, attn_o, lru_o, ga, gb, w_ba, w_bl, w_out, norm2_g):
    m, d = x.shape
    tm = _row_block(m, 256)
    row = lambda width: pl.BlockSpec((tm, width), lambda i: (i, 0))
    return pl.pallas_call(
        _merge_kernel,
        grid=(m // tm,),
        in_specs=[row(d), row(attn_o.shape[1]), row(lru_o.shape[1]), row(d), row(d),
                  _resident(w_ba.shape, lambda i: (0, 0)), _resident(w_bl.shape, lambda i: (0, 0)),
                  _resident(w_out.shape, lambda i: (0, 0)), pl.BlockSpec((1, d), lambda i: (0, 0))],
        out_specs=[row(d), row(d)],
        out_shape=[jax.ShapeDtypeStruct((m, d), F32), jax.ShapeDtypeStruct((m, d), BF16)],
        compiler_params=_params("parallel"),
        name="gated_merge",
    )(x, attn_o, lru_o, ga, gb, w_ba, w_bl, w_out, norm2_g.reshape(1, d))


def _ffn_kernel(h_ref, x1_ref, wg_ref, wu_ref, wd_ref, gf_ref, y_ref, acc_scr):
    f = pl.program_id(1)

    @pl.when(f == 0)
    def _():
        acc_scr[...] = jnp.zeros(acc_scr.shape, F32)

    h = h_ref[...]
    gate = jnp.dot(h, wg_ref[...], preferred_element_type=F32)
    up = jnp.dot(h, wu_ref[...], preferred_element_type=F32)
    act = (jax.nn.silu(gate) * up).astype(BF16)
    acc_scr[...] += jnp.dot(act, wd_ref[...], preferred_element_type=F32)

    @pl.when(f == pl.num_programs(1) - 1)
    def _():
        x2 = x1_ref[...] + acc_scr[...]
        y_ref[...] = x2 * lax.rsqrt(jnp.mean(x2 * x2, axis=-1, keepdims=True) + EPS) * gf_ref[...]


def _ffn(h2, x1, w_gate, w_up, w_down, norm_f_g):
    m, d = x1.shape
    ff = w_gate.shape[1]
    tm = _row_block(m, 512)
    tf = 512
    assert ff % tf == 0
    return pl.pallas_call(
        _ffn_kernel,
        grid=(m // tm, ff // tf),
        in_specs=[pl.BlockSpec((tm, d), lambda i, f: (i, 0)), pl.BlockSpec((tm, d), lambda i, f: (i, 0)),
                  pl.BlockSpec((d, tf), lambda i, f: (0, f)), pl.BlockSpec((d, tf), lambda i, f: (0, f)),
                  pl.BlockSpec((tf, d), lambda i, f: (f, 0)), pl.BlockSpec((1, d), lambda i, f: (0, 0))],
        out_specs=pl.BlockSpec((tm, d), lambda i, f: (i, 0)),
        out_shape=jax.ShapeDtypeStruct((m, d), F32),
        scratch_shapes=[pltpu.VMEM((tm, d), F32)],
        compiler_params=_params("parallel", "arbitrary"),
        name="swiglu_ffn",
    )(h2, x1, w_gate, w_up, w_down, norm_f_g.reshape(1, d))


SCORE_PAGES = 16


def _sample_score_kernel(pt_ref, qi_ref, wi_ref, kin_ref, kidx_hbm, keys_ref, keynew_ref,
                         kibuf, sems, *, n_pages, page, n_idx_heads, idx_dim):
    b = pl.program_id(0)
    slot = b % 2
    idx_scale = (idx_dim ** -0.5) * (n_idx_heads ** -0.5)

    def page_copy(seq, sl, p):
        return pltpu.make_async_copy(kidx_hbm.at[pt_ref[seq, p]], kibuf.at[sl, p], sems.at[sl])

    def start_all(seq, sl):
        def body(p, c):
            page_copy(seq, sl, p).start()
            return c

        lax.fori_loop(0, n_pages, body, 0, unroll=DMA_UNROLL)

    @pl.when(b == 0)
    def _():
        start_all(0, 0)

    @pl.when(b + 1 < pl.num_programs(0))
    def _():
        start_all(b + 1, 1 - slot)

    def wait(p, c):
        page_copy(b, slot, p).wait()
        return c

    lax.fori_loop(0, n_pages, wait, 0, unroll=DMA_UNROLL)

    qi = qi_ref[0]
    wcol = wi_ref[0] * idx_scale
    wb = jnp.broadcast_to(wcol, (n_idx_heads, page))

    def score(g, c):
        p0 = pl.multiple_of(g * SCORE_PAGES, SCORE_PAGES)
        pages = kibuf[slot, pl.ds(p0, SCORE_PAGES)]
        kblk = jnp.concatenate([pages[j] for j in range(SCORE_PAGES)], axis=1).astype(BF16)
        s = jnp.dot(qi, kblk, preferred_element_type=F32)
        rows = []
        for j in range(SCORE_PAGES):
            sj = jnp.maximum(s[:, j * page:(j + 1) * page], 0.0) * wb
            rows.append(jnp.sum(sj, axis=0, keepdims=True))
        keys_ref[0, pl.ds(p0, SCORE_PAGES), :] = _sortable_key(jnp.concatenate(rows, axis=0))
        return c

    lax.fori_loop(0, n_pages // SCORE_PAGES, score, 0)

    s_new = jnp.sum(qi.astype(F32) * kin_ref[0].astype(BF16).astype(F32), axis=1, keepdims=True)
    i_new = jnp.sum(jnp.maximum(s_new, 0.0) * wcol, axis=0, keepdims=True)
    keynew_ref[0] = jnp.broadcast_to(_sortable_key(i_new), (1, LANES))


def _sample_pick_kernel(keys_ref, keynew_ref, pt_ref, rows_ref, sel_scr, selnew_scr,
                        *, n_pages, page, n_sel):
    bsz = keys_ref.shape[0]
    key_new = keynew_ref[...][:, :, :1]

    def count_ge(cand):
        cnt = jnp.sum(jnp.where(keys_ref[...] >= cand, 1.0, 0.0), axis=1, keepdims=True)
        cnt = jnp.sum(cnt, axis=2, keepdims=True)
        return cnt + jnp.where(key_new >= cand, 1.0, 0.0)

    thr = _kth_largest_key(count_ge, n_sel, (bsz, 1, 1))
    sel_scr[...] = jnp.where(keys_ref[...] >= thr, 1.0, 0.0)
    selnew_scr[...] = jnp.broadcast_to(jnp.where(key_new >= thr, 1, 0), selnew_scr.shape)

    tri_o = jnp.where(lax.broadcasted_iota(I32, (page, page), 1) <= lax.broadcasted_iota(I32, (page, page), 0),
                      1.0, 0.0).astype(BF16)
    tri_p = jnp.where(lax.broadcasted_iota(I32, (n_pages, n_pages), 1)
                      <= lax.broadcasted_iota(I32, (n_pages, n_pages), 0), 1.0, 0.0).astype(BF16)
    j_row = lax.broadcasted_iota(I32, (1, n_sel), 1).astype(F32)
    p_col = lax.broadcasted_iota(I32, (n_pages, 1), 0).astype(F32)
    reps = n_sel // LANES

    def invert(b, c):
        sel = sel_scr[b]
        cnt = jnp.sum(sel, axis=1, keepdims=True)
        cnt_b = jnp.broadcast_to(cnt, (n_pages, LANES))
        cum_b = jnp.dot(tri_p, cnt_b.astype(BF16), preferred_element_type=F32)
        base_b = cum_b - cnt_b
        tile = lambda a: jnp.concatenate([a] * reps, axis=1)
        page_of = jnp.sum(jnp.where(tile(cum_b) <= j_row, 1.0, 0.0), axis=0, keepdims=True)
        onehot = jnp.where(p_col == page_of, 1.0, 0.0)
        base_j = jnp.sum(onehot * tile(base_b), axis=0, keepdims=True)
        pt_col = pt_ref[b].astype(F32)
        phys_j = jnp.sum(onehot * pt_col, axis=0, keepdims=True)
        incl_t = lax.dot_general(tri_o, sel.astype(BF16), _NT_DIMS, preferred_element_type=F32)
        incl_j = jnp.dot(incl_t.astype(BF16), onehot.astype(BF16), preferred_element_type=F32)
        off_j = jnp.sum(jnp.where(incl_j <= j_row - base_j, 1.0, 0.0), axis=0, keepdims=True)
        row = (phys_j * page + off_j).astype(I32)
        total = jnp.sum(cnt, axis=0, keepdims=True)
        is_new = (selnew_scr[b][:1, :1] > 0) & (j_row == n_sel - 1)
        rows_ref[pl.ds(b, 1), :] = jnp.where(is_new | (j_row >= total), -1, row)
        return c

    lax.fori_loop(0, bsz, invert, 0)


def _sample_select(qi, wi, ki_new, cache_kidx, page_table, *, n_sel):
    bsz, n_idx_heads, idx_dim = qi.shape
    n_pages = page_table.shape[1]
    n_pool, page, _ = cache_kidx.shape
    assert page == LANES and n_pages % SCORE_PAGES == 0 and n_sel % LANES == 0
    assert n_pool * page < 2 ** 24
    kidx_t = jnp.swapaxes(cache_kidx, 1, 2)
    keys, key_new = pl.pallas_call(
        functools.partial(_sample_score_kernel, n_pages=n_pages, page=page, n_idx_heads=n_idx_heads,
                          idx_dim=idx_dim),
        grid_spec=pltpu.PrefetchScalarGridSpec(
            num_scalar_prefetch=1,
            grid=(bsz,),
            in_specs=[pl.BlockSpec((1, n_idx_heads, idx_dim), lambda b, pt: (b, 0, 0)),
                      pl.BlockSpec((1, n_idx_heads, 1), lambda b, pt: (b, 0, 0)),
                      pl.BlockSpec((1, 1, idx_dim), lambda b, pt: (b, 0, 0)),
                      pl.BlockSpec(memory_space=pl.ANY)],
            out_specs=[pl.BlockSpec((1, n_pages, page), lambda b, pt: (b, 0, 0)),
                       pl.BlockSpec((1, 1, LANES), lambda b, pt: (b, 0, 0))],
            scratch_shapes=[pltpu.VMEM((2, n_pages, idx_dim, page), F32), pltpu.SemaphoreType.DMA((2,))],
        ),
        out_shape=[jax.ShapeDtypeStruct((bsz, n_pages, page), I32),
                   jax.ShapeDtypeStruct((bsz, 1, LANES), I32)],
        compiler_params=_params("arbitrary"),
        name="sample_index_scores",
    )(page_table, qi, wi, ki_new, kidx_t)
    return pl.pallas_call(
        functools.partial(_sample_pick_kernel, n_pages=n_pages, page=page, n_sel=n_sel),
        out_shape=jax.ShapeDtypeStruct((bsz, n_sel), I32),
        scratch_shapes=[pltpu.VMEM((bsz, n_pages, page), F32), pltpu.VMEM((bsz, SUBLANES, LANES), I32)],
        compiler_params=pltpu.CompilerParams(vmem_limit_bytes=VMEM_LIMIT_BYTES),
        name="sample_index_pick",
    )(keys, key_new, page_table[:, :, None])


def _sample_attend_kernel(rows_sm, q_ref, kn_ref, vn_ref, ck_hbm, cv_hbm, o_ref,
                          kbuf, vbuf, sems, *, n_sel, n_heads, head_dim):
    b = pl.program_id(0)
    slot = b % 2

    def copies(seq, sl, j):
        row = jnp.maximum(rows_sm[seq, j], 0)
        return (pltpu.make_async_copy(ck_hbm.at[row], kbuf.at[sl, j], sems.at[0, sl]),
                pltpu.make_async_copy(cv_hbm.at[row], vbuf.at[sl, j], sems.at[1, sl]))

    def start_all(seq, sl):
        def body(j, c):
            ck, cv = copies(seq, sl, j)
            ck.start()
            cv.start()
            return c

        lax.fori_loop(0, n_sel, body, 0, unroll=DMA_UNROLL)

    @pl.when(b == 0)
    def _():
        start_all(0, 0)

    @pl.when(b + 1 < pl.num_programs(0))
    def _():
        start_all(b + 1, 1 - slot)

    def wait(j, c):
        ck, cv = copies(b, slot, j)
        ck.wait()
        cv.wait()
        return c

    lax.fori_loop(0, n_sel, wait, 0, unroll=DMA_UNROLL)

    @pl.when(rows_sm[b, n_sel - 1] < 0)
    def _():
        kbuf[slot, n_sel - 1] = kn_ref[0]
        vbuf[slot, n_sel - 1] = vn_ref[0]

    q = q_ref[0].astype(F32)
    s = jnp.sum(kbuf[slot] * q[None], axis=2, keepdims=True)
    p = jnp.exp2(s - jnp.max(s, axis=0, keepdims=True))
    o = jnp.sum(p * vbuf[slot], axis=0) / jnp.sum(p, axis=0)
    o_ref[0] = o.astype(o_ref.dtype)


def _sample_attend(rows, q, k_new, v_new, cache_k, cache_v):
    bsz, n_sel = rows.shape
    n_pool, page, n_heads, head_dim = cache_k.shape
    kern = functools.partial(_sample_attend_kernel, n_sel=n_sel, n_heads=n_heads, head_dim=head_dim)
    per_seq = lambda: pl.BlockSpec((1, n_heads, head_dim), lambda b, r_sm: (b, 0, 0))
    flat = lambda c: c.reshape(n_pool * page, n_heads, head_dim)
    return pl.pallas_call(
        kern,
        grid_spec=pltpu.PrefetchScalarGridSpec(
            num_scalar_prefetch=1,
            grid=(bsz,),
            in_specs=[per_seq(), per_seq(), per_seq(),
                      pl.BlockSpec(memory_space=pl.ANY), pl.BlockSpec(memory_space=pl.ANY)],
            out_specs=per_seq(),
            scratch_shapes=[pltpu.VMEM((2, n_sel, n_heads, head_dim), F32),
                            pltpu.VMEM((2, n_sel, n_heads, head_dim), F32),
                            pltpu.SemaphoreType.DMA((2, 2))],
        ),
        out_shape=jax.ShapeDtypeStruct((bsz, n_heads, head_dim), BF16),
        compiler_params=_params("arbitrary"),
        name="sample_sparse_attention",
    )(rows, q, k_new, v_new, flat(cache_k), flat(cache_v))


def kernel(x_prompt, x_sample, cache_k, cache_v, cache_kidx, state_conv, state_rglru, page_table,
           norm1_g, w_in, conv_w, conv_b, rg_wa, rg_ba, rg_wx, rg_bx, rg_lambda,
           w_branch_attn, w_branch_lru, w_out, norm2_g, w_ffn_gate, w_ffn_up, w_ffn_down, norm_f_g):
    bp, seq, d_model = x_prompt.shape
    bd, dec_seq, _ = x_sample.shape
    _, page, n_heads, head_dim = cache_k.shape
    idx_dim = cache_kidx.shape[2]
    lru_w = conv_w.shape[1]
    conv_width = conv_w.shape[0]
    attn_w = n_heads * head_dim
    n_idx_heads = (w_in.shape[1] - 3 * attn_w - idx_dim - 2 * lru_w - 2 * d_model) // (idx_dim + 1)
    assert bp == 1 and dec_seq == 1
    assert idx_dim + n_idx_heads <= LANES

    sizes = [attn_w, attn_w, attn_w, n_idx_heads * idx_dim, idx_dim + n_idx_heads, lru_w, lru_w,
             d_model, d_model]
    cuts = np.cumsum([0] + sizes)
    w_q, w_k, w_v, w_qi, w_kiwi, w_xl, w_yl, w_ga, w_gb = (
        w_in[:, cuts[n]:cuts[n + 1]].astype(BF16) for n in range(len(sizes)))
    w_kiwi = jnp.pad(w_kiwi, ((0, 0), (0, LANES - w_kiwi.shape[1])))
    tail_w = (w_branch_attn.astype(BF16), w_branch_lru.astype(BF16), w_out.astype(BF16), norm2_g)
    ffn_w = (w_ffn_gate.astype(BF16), w_ffn_up.astype(BF16), w_ffn_down.astype(BF16), norm_f_g)
    lru_p = (conv_w, conv_b, rg_wa, rg_ba, rg_wx, rg_bx, rg_lambda)

    def project(x2d, by_head):
        hd = head_dim if by_head else None
        h = _rmsnorm_bf16(x2d, norm1_g)
        (q,) = _matmul(h, w_q, [BF16], out_scale=head_dim ** -0.5 * LOG2_E, head_dim=hd)
        k, k_bf = _matmul(h, w_k, [F32, BF16], head_dim=hd)
        v, v_bf = _matmul(h, w_v, [F32, BF16], head_dim=hd)
        (qi,) = _matmul(h, w_qi, [BF16])
        (kiwi,) = _matmul(h, w_kiwi, [F32])
        (x_lru,) = _matmul(h, w_xl, [F32])
        (y_lru,) = _matmul(h, w_yl, [F32])
        (ga,) = _matmul(h, w_ga, [F32])
        (gb,) = _matmul(h, w_gb, [F32])
        return q, k, k_bf, v, v_bf, qi, kiwi, x_lru, y_lru, ga, gb

    def finish(x2d, attn_o, lru_o, ga, gb):
        x1, h2 = _merge(x2d, attn_o, lru_o, ga, gb, *tail_w)
        return _ffn(h2, x1, *ffn_w)

    xp = x_prompt.reshape(seq, d_model)
    q, k_p, k_bf, v_p, v_bf, qi, kiwi, x_lru, y_lru, ga, gb = project(xp, by_head=True)
    ki_p = kiwi[:, :idx_dim]
    attn_p = _prompt_attention(q, qi, kiwi, ki_p.astype(BF16), k_bf, v_bf, n_heads=n_heads,
                               head_dim=head_dim, n_idx_heads=n_idx_heads, idx_dim=idx_dim)
    lru_o, h_p = _lru_prompt(x_lru, y_lru, *lru_p)
    y_prompt = finish(xp, attn_p, lru_o, ga, gb).reshape(bp, seq, d_model)
    conv_p = x_lru[seq - (conv_width - 1):].reshape(bp, conv_width - 1, lru_w)

    xs = x_sample.reshape(bd, d_model)
    q, k_s, _, v_s, _, qi, kiwi, x_lru, y_lru, ga, gb = project(xs, by_head=False)
    ki_s = kiwi[:, :idx_dim]
    wi_s = kiwi[:, idx_dim:idx_dim + n_idx_heads]
    past = page_table.shape[1] * page
    n_sel = min(TOPK_MAX, (past + dec_seq) // 4)
    rows = _sample_select(qi.reshape(bd, n_idx_heads, idx_dim), wi_s[:, :, None], ki_s[:, None, :],
                          cache_kidx, page_table, n_sel=n_sel)
    hd = (bd, n_heads, head_dim)
    attn_s = _sample_attend(rows, q.reshape(hd), k_s.reshape(hd), v_s.reshape(hd), cache_k, cache_v)
    lru_o, h_s = _lru_step(x_lru, y_lru, jnp.swapaxes(state_conv, 0, 1), state_rglru, *lru_p)
    y_sample = finish(xs, attn_s.reshape(bd, attn_w), lru_o, ga, gb).reshape(bd, dec_seq, d_model)
    conv_s = jnp.concatenate([state_conv[:, 1:], x_lru[:, None, :]], axis=1)

    return (y_prompt, y_sample,
            k_p.reshape(bp, seq, n_heads, head_dim), v_p.reshape(bp, seq, n_heads, head_dim),
            ki_p.reshape(bp, seq, idx_dim), conv_p, h_p.reshape(bp, lru_w),
            k_s.reshape(bd, dec_seq, n_heads, head_dim), v_s.reshape(bd, dec_seq, n_heads, head_dim),
            ki_s.reshape(bd, dec_seq, idx_dim), conv_s, h_s)
```

```python
import functools

import jax
import jax.numpy as jnp
import numpy as np
from jax import lax
from jax.experimental import pallas as pl
from jax.experimental.pallas import tpu as pltpu

TOPK_MAX = 256
LRU_C = 8.0
EPS = 1e-6
MASK_VALUE = -1e30
LOG2_E = 1.4426950408889634

V7X_VMEM_BYTES = 64 * 1024 * 1024
VMEM_LIMIT_BYTES = 60 * 1024 * 1024
LANES = 128
SUBLANES = 8

BF16 = jnp.bfloat16
F32 = jnp.float32
I32 = jnp.int32
INT32_MIN = -(2 ** 31)

_NT_DIMS = (((1,), (1,)), ((), ()))


def _params(*sem):
    return pltpu.CompilerParams(dimension_semantics=sem, vmem_limit_bytes=VMEM_LIMIT_BYTES)


def _resident(block_shape, index_map):
    return pl.BlockSpec(block_shape, index_map, pipeline_mode=pl.Buffered(1))


def _row_block(m, target):
    tm = min(m, target)
    assert m % tm == 0, (m, tm)
    return tm


def _rmsnorm_kernel(x_ref, g_ref, o_ref):
    x = x_ref[...]
    y = x * lax.rsqrt(jnp.mean(x * x, axis=-1, keepdims=True) + EPS) * g_ref[...]
    o_ref[...] = y.astype(o_ref.dtype)


def _rmsnorm_bf16(x, g):
    m, d = x.shape
    tm = _row_block(m, 512)
    return pl.pallas_call(
        _rmsnorm_kernel,
        grid=(m // tm,),
        in_specs=[pl.BlockSpec((tm, d), lambda i: (i, 0)), pl.BlockSpec((1, d), lambda i: (0, 0))],
        out_specs=pl.BlockSpec((tm, d), lambda i: (i, 0)),
        out_shape=jax.ShapeDtypeStruct((m, d), BF16),
        compiler_params=_params("parallel"),
        name="rmsnorm_bf16",
    )(x, g.reshape(1, d))


def _matmul_kernel(a_ref, w_ref, *o_refs, out_scale):
    acc = jnp.dot(a_ref[...], w_ref[...], preferred_element_type=F32)
    if out_scale is not None:
        acc = acc * out_scale
    for o_ref in o_refs:
        if len(o_ref.shape) == 3:
            hd = o_ref.shape[2]
            for h in range(o_ref.shape[0]):
                o_ref[h] = acc[:, h * hd:(h + 1) * hd].astype(o_ref.dtype)
        else:
            o_ref[...] = acc.astype(o_ref.dtype)


def _matmul(a, w, out_dtypes, out_scale=None, head_dim=None):
    m, k = a.shape
    n = w.shape[1]
    tm = _row_block(m, 1024)
    tn = _row_block(n, 1024)
    by_head = [head_dim is not None and dt == BF16 for dt in out_dtypes]
    assert head_dim is None or tn % head_dim == 0
    outs = pl.pallas_call(
        functools.partial(_matmul_kernel, out_scale=out_scale),
        grid=(n // tn, m // tm),
        in_specs=[pl.BlockSpec((tm, k), lambda j, i: (i, 0)), pl.BlockSpec((k, tn), lambda j, i: (0, j))],
        out_specs=[pl.BlockSpec((tn // head_dim, tm, head_dim), lambda j, i: (j, i, 0)) if hm
                   else pl.BlockSpec((tm, tn), lambda j, i: (i, j)) for hm in by_head],
        out_shape=[jax.ShapeDtypeStruct((n // head_dim, m, head_dim) if hm else (m, n), dt)
                   for hm, dt in zip(by_head, out_dtypes)],
        compiler_params=_params("parallel", "parallel"),
        name="proj_matmul",
    )(a, w)
    return outs


ORDER_NEG_INF = -(2 ** 31) + 2 ** 23 - 1


def _float_at(order):
    bits = order ^ ((order >> 31) & jnp.int32(0x7FFFFFFF))
    return jnp.where(order < ORDER_NEG_INF, -jnp.inf, pltpu.bitcast(bits, F32))


def _kth_largest(count_ge, k, shape):
    def body(it, order):
        cand = order + (jnp.int32(1) << (jnp.int32(31) - it))
        return jnp.where(count_ge(_float_at(cand)) >= k, cand, order)

    return _float_at(lax.fori_loop(0, 32, body, jnp.full(shape, INT32_MIN, I32)))


IDX_SUB = 256
KV_CHUNK = 512
ATT_CHUNK = 256
ROW_SUB = 128


def _prompt_attn_kernel(q_ref, qi_ref, kiwi_ref, ki_ref, k_ref, v_ref, o_ref,
                        wb_scr, keys_scr, thr_scr, bias_scr, m_scr, l_scr, acc_scr,
                        *, tq, n_heads, head_dim, n_idx_heads, idx_dim, n_sel, max_chunks):
    i = pl.program_id(0)
    row0 = i * tq
    n_chunks = (row0 + tq + KV_CHUNK - 1) // KV_CHUNK
    idx_scale = (idx_dim ** -0.5) * (n_idx_heads ** -0.5)
    rows = row0 + lax.broadcasted_iota(I32, (tq, 1), 0)

    wi = kiwi_ref[:, idx_dim:idx_dim + n_idx_heads] * idx_scale
    for h in range(n_idx_heads):
        wb_scr[h] = jnp.broadcast_to(wi[:, h:h + 1], (tq, LANES))

    lane_reps = KV_CHUNK // LANES

    def score_chunk(c, carry):
        for sub in range(KV_CHUNK // IDX_SUB):
            col0 = pl.multiple_of(c * KV_CHUNK + sub * IDX_SUB, IDX_SUB)
            ki_c = ki_ref[pl.ds(col0, IDX_SUB), :]
            acc = jnp.zeros((tq, IDX_SUB), F32)
            for h in range(n_idx_heads):
                s = lax.dot_general(qi_ref[:, h * idx_dim:(h + 1) * idx_dim], ki_c, _NT_DIMS,
                                    preferred_element_type=F32)
                wb = wb_scr[h]
                acc = acc + jnp.maximum(s, 0.0) * jnp.concatenate([wb] * (IDX_SUB // LANES), axis=1)
            cols = col0 + lax.broadcasted_iota(I32, (1, IDX_SUB), 1)
            score = jnp.where(cols <= rows, jnp.where(acc == 0.0, 0.0, acc), -jnp.inf)
            blk0 = c * lane_reps + sub * (IDX_SUB // LANES)
            for j in range(IDX_SUB // LANES):
                keys_scr[blk0 + j] = score[:, j * LANES:(j + 1) * LANES]
        return carry

    lax.fori_loop(0, n_chunks, score_chunk, 0)

    def count_ge(cand):
        cand_b = jnp.broadcast_to(cand, (tq, LANES))

        def counter(n):
            def run():
                parts = []
                for rb in range(tq // ROW_SUB):
                    rs = slice(rb * ROW_SUB, (rb + 1) * ROW_SUB)
                    acc = jnp.zeros((ROW_SUB, LANES), I32)
                    for blk in range(n * lane_reps):
                        acc = acc + jnp.where(keys_scr[blk, rs, :] >= cand_b[rs], 1, 0)
                    parts.append(acc)
                return jnp.concatenate(parts, axis=0)
            return run

        acc = lax.switch(n_chunks - 1, [counter(n) for n in range(1, max_chunks + 1)])
        return jnp.sum(acc.astype(F32), axis=1, keepdims=True).astype(I32)

    thr_scr[...] = jnp.broadcast_to(_kth_largest(count_ge, n_sel, (tq, 1)), (tq, LANES))

    m_scr[...] = jnp.full(m_scr.shape, MASK_VALUE, F32)
    l_scr[...] = jnp.zeros(l_scr.shape, F32)
    acc_scr[...] = jnp.zeros(acc_scr.shape, F32)

    att_reps = ATT_CHUNK // LANES

    def attend_chunk(c, carry):
        col0 = pl.multiple_of(c * ATT_CHUNK, ATT_CHUNK)
        cols = col0 + lax.broadcasted_iota(I32, (1, ATT_CHUNK), 1)
        thr_b = jnp.concatenate([thr_scr[...]] * att_reps, axis=1)
        keys = jnp.concatenate([keys_scr[c * att_reps + j] for j in range(att_reps)], axis=1)
        sel = (keys >= thr_b) & (cols <= rows)
        bias_scr[...] = jnp.where(sel, 0.0, MASK_VALUE)
        for h in range(n_heads):
            s = lax.dot_general(q_ref[h], k_ref[h, pl.ds(col0, ATT_CHUNK), :], _NT_DIMS,
                                preferred_element_type=F32) + bias_scr[...]
            m_prev = m_scr[h]
            m_new = jnp.maximum(m_prev, jnp.max(s, axis=1, keepdims=True))
            alpha = jnp.exp2(m_prev - m_new)
            p = jnp.exp2(s - jnp.concatenate([m_new] * att_reps, axis=1))
            l_scr[h] = alpha * l_scr[h] + jnp.sum(p, axis=1, keepdims=True)
            acc_scr[h] = alpha * acc_scr[h] + jnp.dot(p.astype(BF16), v_ref[h, pl.ds(col0, ATT_CHUNK), :],
                                                      preferred_element_type=F32)
            m_scr[h] = m_new
        return carry

    lax.fori_loop(0, (row0 + tq + ATT_CHUNK - 1) // ATT_CHUNK, attend_chunk, 0)

    for h in range(n_heads):
        o_ref[:, h * head_dim:(h + 1) * head_dim] = (acc_scr[h] / l_scr[h]).astype(o_ref.dtype)


def _prompt_attention(q, qi, kiwi, ki, k, v, *, n_heads, head_dim, n_idx_heads, idx_dim):
    t = q.shape[1]
    tq = _row_block(t, 256)
    assert t % KV_CHUNK == 0 and t >= TOPK_MAX * 2 and head_dim == LANES
    n_sel = min(TOPK_MAX, t // 4)
    aw = n_heads * head_dim
    kern = functools.partial(_prompt_attn_kernel, tq=tq, n_heads=n_heads, head_dim=head_dim,
                             n_idx_heads=n_idx_heads, idx_dim=idx_dim, n_sel=n_sel,
                             max_chunks=t // KV_CHUNK)
    return pl.pallas_call(
        kern,
        grid=(t // tq,),
        in_specs=[
            pl.BlockSpec((n_heads, tq, head_dim), lambda i: (0, i, 0)),
            pl.BlockSpec((tq, n_idx_heads * idx_dim), lambda i: (i, 0)),
            pl.BlockSpec((tq, LANES), lambda i: (i, 0)),
            _resident((t, idx_dim), lambda i: (0, 0)),
            _resident((n_heads, t, head_dim), lambda i: (0, 0, 0)),
            _resident((n_heads, t, head_dim), lambda i: (0, 0, 0)),
        ],
        out_specs=pl.BlockSpec((tq, aw), lambda i: (i, 0)),
        out_shape=jax.ShapeDtypeStruct((t, aw), BF16),
        scratch_shapes=[
            pltpu.VMEM((n_idx_heads, tq, LANES), F32),
            pltpu.VMEM((t // LANES, tq, LANES), F32),
            pltpu.VMEM((tq, LANES), F32),
            pltpu.VMEM((tq, ATT_CHUNK), F32),
            pltpu.VMEM((n_heads, tq, LANES), F32),
            pltpu.VMEM((n_heads, tq, LANES), F32),
            pltpu.VMEM((n_heads, tq, head_dim), F32),
        ],
        compiler_params=_params("arbitrary"),
        name="prompt_sparse_attention",
    )(q, qi, kiwi, ki, k, v)


def _softplus(x):
    return jnp.maximum(x, 0.0) + jnp.log1p(jnp.exp(-jnp.abs(x)))


def _split_bf16(x):
    hi = x.astype(BF16)
    lo = (x - hi.astype(F32)).astype(BF16)
    return hi, lo


def _lru_gates(xc, wa_ref, ba_ref, wx_ref, bx_ref, lam_ref):
    n_blocks, blk, _ = wa_ref.shape
    ra, rx = [], []
    for n in range(n_blocks):
        x_hi, x_lo = _split_bf16(xc[:, n * blk:(n + 1) * blk])
        for w_ref, dst in ((wa_ref, ra), (wx_ref, rx)):
            w_hi, w_lo = _split_bf16(w_ref[n])
            dst.append(jnp.dot(x_hi, w_hi, preferred_element_type=F32)
                       + jnp.dot(x_hi, w_lo, preferred_element_type=F32)
                       + jnp.dot(x_lo, w_hi, preferred_element_type=F32))
    r = jax.nn.sigmoid(jnp.concatenate(ra, axis=1) + ba_ref[...])
    gate_i = jax.nn.sigmoid(jnp.concatenate(rx, axis=1) + bx_ref[...])
    log_a = (-LRU_C) * r * _softplus(-lam_ref[...])
    a = jnp.exp(log_a)
    b = jnp.sqrt(jnp.tanh(-log_a) * (1.0 + a * a)) * (gate_i * xc)
    return a, b


SCAN_UNROLL = 8
DMA_UNROLL = 8


def _lru_prompt_kernel(x_ref, y_ref, cw_ref, cb_ref, wa_ref, ba_ref, wx_ref, bx_ref, lam_ref,
                       o_ref, hlast_ref, ext_scr, a_scr, b_scr, h_scr, *, tb, conv_width):
    t = pl.program_id(0)
    pad = SUBLANES
    assert conv_width - 1 <= pad

    @pl.when(t == 0)
    def _():
        ext_scr[0:pad, :] = jnp.zeros((pad, ext_scr.shape[1]), F32)
        hlast_ref[...] = jnp.zeros(hlast_ref.shape, F32)

    ext_scr[pad:pad + tb, :] = x_ref[...]
    conv = None
    for j in range(conv_width):
        term = cw_ref[j:j + 1, :] * ext_scr[pl.ds(pad - (conv_width - 1) + j, tb), :]
        conv = term if conv is None else conv + term
    xc = cb_ref[...] + conv
    a, b = _lru_gates(xc, wa_ref, ba_ref, wx_ref, bx_ref, lam_ref)
    a_scr[...] = a
    b_scr[...] = b

    sub = lax.broadcasted_iota(I32, (SUBLANES, 1), 0)

    def tile(j, h_prev):
        r0 = pl.multiple_of(j * SUBLANES, SUBLANES)
        at = a_scr[pl.ds(r0, SUBLANES), :]
        bt = b_scr[pl.ds(r0, SUBLANES), :]
        for d in (1, 2, 4):
            a_sh = pltpu.roll(at, d, 0)
            b_sh = pltpu.roll(bt, d, 0)
            ok = sub >= d
            bt = jnp.where(ok, at * b_sh + bt, bt)
            at = jnp.where(ok, at * a_sh, at)
        h = at * h_prev + bt
        h_scr[pl.ds(r0, SUBLANES), :] = h
        return jnp.broadcast_to(h[SUBLANES - 1:SUBLANES, :], h.shape)

    h_last = lax.fori_loop(0, tb // SUBLANES, tile, hlast_ref[...], unroll=SCAN_UNROLL)
    hlast_ref[...] = h_last
    ext_scr[0:pad, :] = ext_scr[tb:tb + pad, :]
    o_ref[...] = (h_scr[...] * jax.nn.gelu(y_ref[...])).astype(o_ref.dtype)


def _lru_prompt(x_lru, y_lru, conv_w, conv_b, rg_wa, rg_ba, rg_wx, rg_bx, rg_lambda):
    t, w = x_lru.shape
    tb = _row_block(t, 256)
    cw = conv_w.shape[0]
    vec = lambda a: a.reshape(1, w)
    full = lambda shape: pl.BlockSpec(shape, lambda i: (0,) * len(shape))
    out, h_last = pl.pallas_call(
        functools.partial(_lru_prompt_kernel, tb=tb, conv_width=cw),
        grid=(t // tb,),
        in_specs=[pl.BlockSpec((tb, w), lambda i: (i, 0)), pl.BlockSpec((tb, w), lambda i: (i, 0)),
                  full((cw, w)), full((1, w)), full(rg_wa.shape), full((1, w)), full(rg_wx.shape),
                  full((1, w)), full((1, w))],
        out_specs=[pl.BlockSpec((tb, w), lambda i: (i, 0)), full((SUBLANES, w))],
        out_shape=[jax.ShapeDtypeStruct((t, w), BF16), jax.ShapeDtypeStruct((SUBLANES, w), F32)],
        scratch_shapes=[pltpu.VMEM((tb + SUBLANES, w), F32), pltpu.VMEM((tb, w), F32),
                        pltpu.VMEM((tb, w), F32), pltpu.VMEM((tb, w), F32)],
        compiler_params=_params("arbitrary"),
        name="rglru_prompt",
    )(x_lru, y_lru, conv_w, vec(conv_b), rg_wa, vec(rg_ba), rg_wx, vec(rg_bx), vec(rg_lambda))
    return out, h_last[0]


def _lru_step_kernel(x_ref, y_ref, sc_ref, h0_ref, cw_ref, cb_ref, wa_ref, ba_ref, wx_ref, bx_ref,
                     lam_ref, o_ref, h_ref, *, conv_width):
    conv = None
    for j in range(conv_width):
        src = x_ref[...] if j == conv_width - 1 else sc_ref[j]
        term = cw_ref[j:j + 1, :] * src
        conv = term if conv is None else conv + term
    xc = cb_ref[...] + conv
    a, b = _lru_gates(xc, wa_ref, ba_ref, wx_ref, bx_ref, lam_ref)
    h = b + a * h0_ref[...]
    h_ref[...] = h
    o_ref[...] = (h * jax.nn.gelu(y_ref[...])).astype(o_ref.dtype)


def _lru_step(x_lru, y_lru, state_conv_t, h0, conv_w, conv_b, rg_wa, rg_ba, rg_wx, rg_bx, rg_lambda):
    b, w = x_lru.shape
    vec = lambda a: a.reshape(1, w)
    return pl.pallas_call(
        functools.partial(_lru_step_kernel, conv_width=conv_w.shape[0]),
        out_shape=[jax.ShapeDtypeStruct((b, w), BF16), jax.ShapeDtypeStruct((b, w), F32)],
        compiler_params=pltpu.CompilerParams(vmem_limit_bytes=VMEM_LIMIT_BYTES),
        name="rglru_step",
    )(x_lru, y_lru, state_conv_t, h0, conv_w, vec(conv_b), rg_wa, vec(rg_ba), rg_wx, vec(rg_bx),
      vec(rg_lambda))


def _merge_kernel(x_ref, ao_ref, lo_ref, ga_ref, gb_ref, wba_ref, wbl_ref, wout_ref, g2_ref,
                  x1_ref, h2_ref):
    attn = jnp.dot(ao_ref[...], wba_ref[...], preferred_element_type=F32)
    lru = jnp.dot(lo_ref[...], wbl_ref[...], preferred_element_type=F32)
    merged = jax.nn.sigmoid(ga_ref[...]) * attn + jax.nn.sigmoid(gb_ref[...]) * lru
    x1 = x_ref[...] + jnp.dot(merged.astype(BF16), wout_ref[...], preferred_element_type=F32)
    x1_ref[...] = x1
    h2 = x1 * lax.rsqrt(jnp.mean(x1 * x1, axis=-1, keepdims=True) + EPS) * g2_ref[...]
    h2_ref[...] = h2.astype(h2_ref.dtype)


def _merge(x, attn_o, lru_o, ga, gb, w_ba, w_bl, w_out, norm2_g):
    m, d = x.shape
    tm = _row_block(m, 256)
    row = lambda width: pl.BlockSpec((tm, width), lambda i: (i, 0))
    return pl.pallas_call(
        _merge_kernel,
        grid=(m // tm,),
        in_specs=[row(d), row(attn_o.shape[1]), row(lru_o.shape[1]), row(d), row(d),
                  _resident(w_ba.shape, lambda i: (0, 0)), _resident(w_bl.shape, lambda i: (0, 0)),
                  _resident(w_out.shape, lambda i: (0, 0)), pl.BlockSpec((1, d), lambda i: (0, 0))],
        out_specs=[row(d), row(d)],
        out_shape=[jax.ShapeDtypeStruct((m, d), F32), jax.ShapeDtypeStruct((m, d), BF16)],
        compiler_params=_params("parallel"),
        name="gated_merge",
    )(x, attn_o, lru_o, ga, gb, w_ba, w_bl, w_out, norm2_g.reshape(1, d))


def _ffn_kernel(h_ref, x1_ref, wg_ref, wu_ref, wd_ref, gf_ref, y_ref, acc_scr):
    f = pl.program_id(1)

    @pl.when(f == 0)
    def _():
        acc_scr[...] = jnp.zeros(acc_scr.shape, F32)

    h = h_ref[...]
    gate = jnp.dot(h, wg_ref[...], preferred_element_type=F32)
    up = jnp.dot(h, wu_ref[...], preferred_element_type=F32)
    act = (jax.nn.silu(gate) * up).astype(BF16)
    acc_scr[...] += jnp.dot(act, wd_ref[...], preferred_element_type=F32)

    @pl.when(f == pl.num_programs(1) - 1)
    def _():
        x2 = x1_ref[...] + acc_scr[...]
        y_ref[...] = x2 * lax.rsqrt(jnp.mean(x2 * x2, axis=-1, keepdims=True) + EPS) * gf_ref[...]


def _ffn(h2, x1, w_gate, w_up, w_down, norm_f_g):
    m, d = x1.shape
    ff = w_gate.shape[1]
    tm = _row_block(m, 512)
    tf = 512
    assert ff % tf == 0
    return pl.pallas_call(
        _ffn_kernel,
        grid=(m // tm, ff // tf),
        in_specs=[pl.BlockSpec((tm, d), lambda i, f: (i, 0)), pl.BlockSpec((tm, d), lambda i, f: (i, 0)),
                  pl.BlockSpec((d, tf), lambda i, f: (0, f)), pl.BlockSpec((d, tf), lambda i, f: (0, f)),
                  pl.BlockSpec((tf, d), lambda i, f: (f, 0)), pl.BlockSpec((1, d), lambda i, f: (0, 0))],
        out_specs=pl.BlockSpec((tm, d), lambda i, f: (i, 0)),
        out_shape=jax.ShapeDtypeStruct((m, d), F32),
        scratch_shapes=[pltpu.VMEM((tm, d), F32)],
        compiler_params=_params("parallel", "arbitrary"),
        name="swiglu_ffn",
    )(h2, x1, w_gate, w_up, w_down, norm_f_g.reshape(1, d))


SCORE_PAGES = 16


def _sample_score_kernel(pt_ref, qi_ref, wi_ref, kin_ref, kidx_hbm, keys_ref, keynew_ref,
                         kibuf, sems, *, n_pages, page, n_idx_heads, idx_dim):
    b = pl.program_id(0)
    slot = b % 2
    idx_scale = (idx_dim ** -0.5) * (n_idx_heads ** -0.5)

    def page_copy(seq, sl, p):
        return pltpu.make_async_copy(kidx_hbm.at[pt_ref[seq, p]], kibuf.at[sl, p], sems.at[sl])

    def start_all(seq, sl):
        def body(p, c):
            page_copy(seq, sl, p).start()
            return c

        lax.fori_loop(0, n_pages, body, 0, unroll=DMA_UNROLL)

    @pl.when(b == 0)
    def _():
        start_all(0, 0)

    @pl.when(b + 1 < pl.num_programs(0))
    def _():
        start_all(b + 1, 1 - slot)

    def wait(p, c):
        page_copy(b, slot, p).wait()
        return c

    lax.fori_loop(0, n_pages, wait, 0, unroll=DMA_UNROLL)

    qi = qi_ref[0]
    wcol = wi_ref[0] * idx_scale
    wb = jnp.broadcast_to(wcol, (n_idx_heads, page))

    def score(g, c):
        p0 = pl.multiple_of(g * SCORE_PAGES, SCORE_PAGES)
        pages = kibuf[slot, pl.ds(p0, SCORE_PAGES)]
        kblk = jnp.concatenate([pages[j] for j in range(SCORE_PAGES)], axis=1).astype(BF16)
        s = jnp.dot(qi, kblk, preferred_element_type=F32)
        rows = []
        for j in range(SCORE_PAGES):
            sj = jnp.maximum(s[:, j * page:(j + 1) * page], 0.0) * wb
            rows.append(jnp.sum(sj, axis=0, keepdims=True))
        scores = jnp.concatenate(rows, axis=0)
        keys_ref[0, pl.ds(p0, SCORE_PAGES), :] = jnp.where(scores == 0.0, 0.0, scores)
        return c

    lax.fori_loop(0, n_pages // SCORE_PAGES, score, 0)

    s_new = jnp.sum(qi.astype(F32) * kin_ref[0].astype(BF16).astype(F32), axis=1, keepdims=True)
    i_new = jnp.sum(jnp.maximum(s_new, 0.0) * wcol, axis=0, keepdims=True)
    keynew_ref[0] = jnp.broadcast_to(jnp.where(i_new == 0.0, 0.0, i_new), (1, LANES))


def _sample_pick_kernel(keys_ref, keynew_ref, pt_ref, rows_ref, sel_scr, selnew_scr,
                        *, n_pages, page, n_sel):
    bsz = keys_ref.shape[0]
    key_new = keynew_ref[...][:, :, :1]

    def count_ge(cand):
        cnt = jnp.sum(jnp.where(keys_ref[...] >= cand, 1.0, 0.0), axis=1, keepdims=True)
        cnt = jnp.sum(cnt, axis=2, keepdims=True)
        return cnt + jnp.where(key_new >= cand, 1.0, 0.0)

    thr = _kth_largest(count_ge, n_sel, (bsz, 1, 1))
    sel_scr[...] = jnp.where(keys_ref[...] >= thr, 1.0, 0.0)
    selnew_scr[...] = jnp.broadcast_to(jnp.where(key_new >= thr, 1, 0), selnew_scr.shape)

    tri_o = jnp.where(lax.broadcasted_iota(I32, (page, page), 1) <= lax.broadcasted_iota(I32, (page, page), 0),
                      1.0, 0.0).astype(BF16)
    tri_p = jnp.where(lax.broadcasted_iota(I32, (n_pages, n_pages), 1)
                      <= lax.broadcasted_iota(I32, (n_pages, n_pages), 0), 1.0, 0.0).astype(BF16)
    j_row = lax.broadcasted_iota(I32, (1, n_sel), 1).astype(F32)
    p_col = lax.broadcasted_iota(I32, (n_pages, 1), 0).astype(F32)
    reps = n_sel // LANES

    def invert(b, c):
        sel = sel_scr[b]
        cnt = jnp.sum(sel, axis=1, keepdims=True)
        cnt_b = jnp.broadcast_to(cnt, (n_pages, LANES))
        cum_b = jnp.dot(tri_p, cnt_b.astype(BF16), preferred_element_type=F32)
        base_b = cum_b - cnt_b
        tile = lambda a: jnp.concatenate([a] * reps, axis=1)
        page_of = jnp.sum(jnp.where(tile(cum_b) <= j_row, 1.0, 0.0), axis=0, keepdims=True)
        onehot = jnp.where(p_col == page_of, 1.0, 0.0)
        base_j = jnp.sum(onehot * tile(base_b), axis=0, keepdims=True)
        pt_col = pt_ref[b].astype(F32)
        phys_j = jnp.sum(onehot * pt_col, axis=0, keepdims=True)
        incl_t = lax.dot_general(tri_o, sel.astype(BF16), _NT_DIMS, preferred_element_type=F32)
        incl_j = jnp.dot(incl_t.astype(BF16), onehot.astype(BF16), preferred_element_type=F32)
        off_j = jnp.sum(jnp.where(incl_j <= j_row - base_j, 1.0, 0.0), axis=0, keepdims=True)
        row = (phys_j * page + off_j).astype(I32)
        total = jnp.sum(cnt, axis=0, keepdims=True)
        is_new = (selnew_scr[b][:1, :1] > 0) & (j_row == n_sel - 1)
        rows_ref[pl.ds(b, 1), :] = jnp.where(is_new | (j_row >= total), -1, row)
        return c

    lax.fori_loop(0, bsz, invert, 0)


def _sample_select(qi, wi, ki_new, cache_kidx, page_table, *, n_sel):
    bsz, n_idx_heads, idx_dim = qi.shape
    n_pages = page_table.shape[1]
    n_pool, page, _ = cache_kidx.shape
    assert page == LANES and n_pages % SCORE_PAGES == 0 and n_sel % LANES == 0
    assert n_pool * page < 2 ** 24
    kidx_t = jnp.swapaxes(cache_kidx, 1, 2)
    keys, key_new = pl.pallas_call(
        functools.partial(_sample_score_kernel, n_pages=n_pages, page=page, n_idx_heads=n_idx_heads,
                          idx_dim=idx_dim),
        grid_spec=pltpu.PrefetchScalarGridSpec(
            num_scalar_prefetch=1,
            grid=(bsz,),
            in_specs=[pl.BlockSpec((1, n_idx_heads, idx_dim), lambda b, pt: (b, 0, 0)),
                      pl.BlockSpec((1, n_idx_heads, 1), lambda b, pt: (b, 0, 0)),
                      pl.BlockSpec((1, 1, idx_dim), lambda b, pt: (b, 0, 0)),
                      pl.BlockSpec(memory_space=pl.ANY)],
            out_specs=[pl.BlockSpec((1, n_pages, page), lambda b, pt: (b, 0, 0)),
                       pl.BlockSpec((1, 1, LANES), lambda b, pt: (b, 0, 0))],
            scratch_shapes=[pltpu.VMEM((2, n_pages, idx_dim, page), F32), pltpu.SemaphoreType.DMA((2,))],
        ),
        out_shape=[jax.ShapeDtypeStruct((bsz, n_pages, page), F32),
                   jax.ShapeDtypeStruct((bsz, 1, LANES), F32)],
        compiler_params=_params("arbitrary"),
        name="sample_index_scores",
    )(page_table, qi, wi, ki_new, kidx_t)
    return pl.pallas_call(
        functools.partial(_sample_pick_kernel, n_pages=n_pages, page=page, n_sel=n_sel),
        out_shape=jax.ShapeDtypeStruct((bsz, n_sel), I32),
        scratch_shapes=[pltpu.VMEM((bsz, n_pages, page), F32), pltpu.VMEM((bsz, SUBLANES, LANES), I32)],
        compiler_params=pltpu.CompilerParams(vmem_limit_bytes=VMEM_LIMIT_BYTES),
        name="sample_index_pick",
    )(keys, key_new, page_table[:, :, None])


def _sample_attend_kernel(rows_sm, q_ref, kn_ref, vn_ref, ck_hbm, cv_hbm, o_ref,
                          kbuf, vbuf, sems, *, n_sel, n_heads, head_dim):
    b = pl.program_id(0)
    slot = b % 2

    def copies(seq, sl, j):
        row = jnp.maximum(rows_sm[seq, j], 0)
        return (pltpu.make_async_copy(ck_hbm.at[row], kbuf.at[sl, j], sems.at[0, sl]),
                pltpu.make_async_copy(cv_hbm.at[row], vbuf.at[sl, j], sems.at[1, sl]))

    def start_all(seq, sl):
        def body(j, c):
            ck, cv = copies(seq, sl, j)
            ck.start()
            cv.start()
            return c

        lax.fori_loop(0, n_sel, body, 0, unroll=DMA_UNROLL)

    @pl.when(b == 0)
    def _():
        start_all(0, 0)

    @pl.when(b + 1 < pl.num_programs(0))
    def _():
        start_all(b + 1, 1 - slot)

    def wait(j, c):
        ck, cv = copies(b, slot, j)
        ck.wait()
        cv.wait()
        return c

    lax.fori_loop(0, n_sel, wait, 0, unroll=DMA_UNROLL)

    @pl.when(rows_sm[b, n_sel - 1] < 0)
    def _():
        kbuf[slot, n_sel - 1] = kn_ref[0]
        vbuf[slot, n_sel - 1] = vn_ref[0]

    q = q_ref[0].astype(F32)
    s = jnp.sum(kbuf[slot] * q[None], axis=2, keepdims=True)
    p = jnp.exp2(s - jnp.max(s, axis=0, keepdims=True))
    o = jnp.sum(p * vbuf[slot], axis=0) / jnp.sum(p, axis=0)
    o_ref[0] = o.astype(o_ref.dtype)


def _sample_attend(rows, q, k_new, v_new, cache_k, cache_v):
    bsz, n_sel = rows.shape
    n_pool, page, n_heads, head_dim = cache_k.shape
    kern = functools.partial(_sample_attend_kernel, n_sel=n_sel, n_heads=n_heads, head_dim=head_dim)
    per_seq = lambda: pl.BlockSpec((1, n_heads, head_dim), lambda b, r_sm: (b, 0, 0))
    flat = lambda c: c.reshape(n_pool * page, n_heads, head_dim)
    return pl.pallas_call(
        kern,
        grid_spec=pltpu.PrefetchScalarGridSpec(
            num_scalar_prefetch=1,
            grid=(bsz,),
            in_specs=[per_seq(), per_seq(), per_seq(),
                      pl.BlockSpec(memory_space=pl.ANY), pl.BlockSpec(memory_space=pl.ANY)],
            out_specs=per_seq(),
            scratch_shapes=[pltpu.VMEM((2, n_sel, n_heads, head_dim), F32),
                            pltpu.VMEM((2, n_sel, n_heads, head_dim), F32),
                            pltpu.SemaphoreType.DMA((2, 2))],
        ),
        out_shape=jax.ShapeDtypeStruct((bsz, n_heads, head_dim), BF16),
        compiler_params=_params("arbitrary"),
        name="sample_sparse_attention",
    )(rows, q, k_new, v_new, flat(cache_k), flat(cache_v))


def kernel(x_prompt, x_sample, cache_k, cache_v, cache_kidx, state_conv, state_rglru, page_table,
           norm1_g, w_in, conv_w, conv_b, rg_wa, rg_ba, rg_wx, rg_bx, rg_lambda,
           w_branch_attn, w_branch_lru, w_out, norm2_g, w_ffn_gate, w_ffn_up, w_ffn_down, norm_f_g):
    bp, seq, d_model = x_prompt.shape
    bd, dec_seq, _ = x_sample.shape
    _, page, n_heads, head_dim = cache_k.shape
    idx_dim = cache_kidx.shape[2]
    lru_w = conv_w.shape[1]
    conv_width = conv_w.shape[0]
    attn_w = n_heads * head_dim
    n_idx_heads = (w_in.shape[1] - 3 * attn_w - idx_dim - 2 * lru_w - 2 * d_model) // (idx_dim + 1)
    assert bp == 1 and dec_seq == 1
    assert idx_dim + n_idx_heads <= LANES

    sizes = [attn_w, attn_w, attn_w, n_idx_heads * idx_dim, idx_dim + n_idx_heads, lru_w, lru_w,
             d_model, d_model]
    cuts = np.cumsum([0] + sizes)
    w_q, w_k, w_v, w_qi, w_kiwi, w_xl, w_yl, w_ga, w_gb = (
        w_in[:, cuts[n]:cuts[n + 1]].astype(BF16) for n in range(len(sizes)))
    w_kiwi = jnp.pad(w_kiwi, ((0, 0), (0, LANES - w_kiwi.shape[1])))
    tail_w = (w_branch_attn.astype(BF16), w_branch_lru.astype(BF16), w_out.astype(BF16), norm2_g)
    ffn_w = (w_ffn_gate.astype(BF16), w_ffn_up.astype(BF16), w_ffn_down.astype(BF16), norm_f_g)
    lru_p = (conv_w, conv_b, rg_wa, rg_ba, rg_wx, rg_bx, rg_lambda)

    def project(x2d, by_head):
        hd = head_dim if by_head else None
        h = _rmsnorm_bf16(x2d, norm1_g)
        (q,) = _matmul(h, w_q, [BF16], out_scale=head_dim ** -0.5 * LOG2_E, head_dim=hd)
        k, k_bf = _matmul(h, w_k, [F32, BF16], head_dim=hd)
        v, v_bf = _matmul(h, w_v, [F32, BF16], head_dim=hd)
        (qi,) = _matmul(h, w_qi, [BF16])
        (kiwi,) = _matmul(h, w_kiwi, [F32])
        (x_lru,) = _matmul(h, w_xl, [F32])
        (y_lru,) = _matmul(h, w_yl, [F32])
        (ga,) = _matmul(h, w_ga, [F32])
        (gb,) = _matmul(h, w_gb, [F32])
        return q, k, k_bf, v, v_bf, qi, kiwi, x_lru, y_lru, ga, gb

    def finish(x2d, attn_o, lru_o, ga, gb):
        x1, h2 = _merge(x2d, attn_o, lru_o, ga, gb, *tail_w)
        return _ffn(h2, x1, *ffn_w)

    xp = x_prompt.reshape(seq, d_model)
    q, k_p, k_bf, v_p, v_bf, qi, kiwi, x_lru, y_lru, ga, gb = project(xp, by_head=True)
    ki_p = kiwi[:, :idx_dim]
    attn_p = _prompt_attention(q, qi, kiwi, ki_p.astype(BF16), k_bf, v_bf, n_heads=n_heads,
                               head_dim=head_dim, n_idx_heads=n_idx_heads, idx_dim=idx_dim)
    lru_o, h_p = _lru_prompt(x_lru, y_lru, *lru_p)
    y_prompt = finish(xp, attn_p, lru_o, ga, gb).reshape(bp, seq, d_model)
    conv_p = x_lru[seq - (conv_width - 1):].reshape(bp, conv_width - 1, lru_w)

    xs = x_sample.reshape(bd, d_model)
    q, k_s, _, v_s, _, qi, kiwi, x_lru, y_lru, ga, gb = project(xs, by_head=False)
    ki_s = kiwi[:, :idx_dim]
    wi_s = kiwi[:, idx_dim:idx_dim + n_idx_heads]
    past = page_table.shape[1] * page
    n_sel = min(TOPK_MAX, (past + dec_seq) // 4)
    rows = _sample_select(qi.reshape(bd, n_idx_heads, idx_dim), wi_s[:, :, None], ki_s[:, None, :],
                          cache_kidx, page_table, n_sel=n_sel)
    hd = (bd, n_heads, head_dim)
    attn_s = _sample_attend(rows, q.reshape(hd), k_s.reshape(hd), v_s.reshape(hd), cache_k, cache_v)
    lru_o, h_s = _lru_step(x_lru, y_lru, jnp.swapaxes(state_conv, 0, 1), state_rglru, *lru_p)
    y_sample = finish(xs, attn_s.reshape(bd, attn_w), lru_o, ga, gb).reshape(bd, dec_seq, d_model)
    conv_s = jnp.concatenate([state_conv[:, 1:], x_lru[:, None, :]], axis=1)

    return (y_prompt, y_sample,
            k_p.reshape(bp, seq, n_heads, head_dim), v_p.reshape(bp, seq, n_heads, head_dim),
            ki_p.reshape(bp, seq, idx_dim), conv_p, h_p.reshape(bp, lru_w),
            k_s.reshape(bd, dec_seq, n_heads, head_dim), v_s.reshape(bd, dec_seq, n_heads, head_dim),
            ki_s.reshape(bd, dec_seq, idx_dim), conv_s, h_s)
```

```python
import functools

import jax
import jax.numpy as jnp
import numpy as np
from jax import lax
from jax.experimental import pallas as pl
from jax.experimental.pallas import tpu as pltpu

TOPK_MAX = 256
LRU_C = 8.0
EPS = 1e-6
MASK_VALUE = -1e30
LOG2_E = 1.4426950408889634

V7X_VMEM_BYTES = 64 * 1024 * 1024
VMEM_LIMIT_BYTES = 60 * 1024 * 1024
LANES = 128
SUBLANES = 8

BF16 = jnp.bfloat16
F32 = jnp.float32
I32 = jnp.int32
INT32_MIN = -(2 ** 31)

_NT_DIMS = (((1,), (1,)), ((), ()))


def _params(*sem):
    return pltpu.CompilerParams(dimension_semantics=sem, vmem_limit_bytes=VMEM_LIMIT_BYTES)


def _resident(block_shape, index_map):
    return pl.BlockSpec(block_shape, index_map, pipeline_mode=pl.Buffered(1))


def _row_block(m, target):
    tm = min(m, target)
    assert m % tm == 0, (m, tm)
    return tm


def _rmsnorm_kernel(x_ref, g_ref, o_ref):
    x = x_ref[...]
    y = x * lax.rsqrt(jnp.mean(x * x, axis=-1, keepdims=True) + EPS) * g_ref[...]
    o_ref[...] = y.astype(o_ref.dtype)


def _rmsnorm_bf16(x, g):
    m, d = x.shape
    tm = _row_block(m, 512)
    return pl.pallas_call(
        _rmsnorm_kernel,
        grid=(m // tm,),
        in_specs=[pl.BlockSpec((tm, d), lambda i: (i, 0)), pl.BlockSpec((1, d), lambda i: (0, 0))],
        out_specs=pl.BlockSpec((tm, d), lambda i: (i, 0)),
        out_shape=jax.ShapeDtypeStruct((m, d), BF16),
        compiler_params=_params("parallel"),
        name="rmsnorm_bf16",
    )(x, g.reshape(1, d))


def _matmul_kernel(a_ref, w_ref, *o_refs, out_scale):
    acc = jnp.dot(a_ref[...], w_ref[...], preferred_element_type=F32)
    if out_scale is not None:
        acc = acc * out_scale
    for o_ref in o_refs:
        if len(o_ref.shape) == 3:
            hd = o_ref.shape[2]
            for h in range(o_ref.shape[0]):
                o_ref[h] = acc[:, h * hd:(h + 1) * hd].astype(o_ref.dtype)
        else:
            o_ref[...] = acc.astype(o_ref.dtype)


def _matmul(a, w, out_dtypes, out_scale=None, head_dim=None):
    m, k = a.shape
    n = w.shape[1]
    tm = _row_block(m, 1024)
    tn = _row_block(n, 1024)
    by_head = [head_dim is not None and dt == BF16 for dt in out_dtypes]
    assert head_dim is None or tn % head_dim == 0
    outs = pl.pallas_call(
        functools.partial(_matmul_kernel, out_scale=out_scale),
        grid=(n // tn, m // tm),
        in_specs=[pl.BlockSpec((tm, k), lambda j, i: (i, 0)), pl.BlockSpec((k, tn), lambda j, i: (0, j))],
        out_specs=[pl.BlockSpec((tn // head_dim, tm, head_dim), lambda j, i: (j, i, 0)) if hm
                   else pl.BlockSpec((tm, tn), lambda j, i: (i, j)) for hm in by_head],
        out_shape=[jax.ShapeDtypeStruct((n // head_dim, m, head_dim) if hm else (m, n), dt)
                   for hm, dt in zip(by_head, out_dtypes)],
        compiler_params=_params("parallel", "parallel"),
        name="proj_matmul",
    )(a, w)
    return outs


ORDER_NEG_INF = -(2 ** 31) + 2 ** 23 - 1


def _float_at(order):
    bits = order ^ ((order >> 31) & jnp.int32(0x7FFFFFFF))
    return jnp.where(order < ORDER_NEG_INF, -jnp.inf, pltpu.bitcast(bits, F32))


def _kth_largest(count_ge, k, shape):
    def unfinished(state):
        it, _, count = state
        return (it < 32) & jnp.any(count != k)

    def body(state):
        it, order, count = state
        cand = order + (jnp.int32(1) << (jnp.int32(31) - it))
        cand_count = count_ge(_float_at(cand))
        take = cand_count >= k
        return it + 1, jnp.where(take, cand, order), jnp.where(take, cand_count, count)

    init = (jnp.int32(0), jnp.full(shape, INT32_MIN, I32), jnp.full(shape, -1, I32))
    return _float_at(lax.while_loop(unfinished, body, init)[1])


IDX_SUB = 256
KV_CHUNK = 512
ATT_CHUNK = 256
ROW_SUB = 128


def _prompt_attn_kernel(q_ref, qi_ref, kiwi_ref, ki_ref, k_ref, v_ref, o_ref,
                        wb_scr, keys_scr, thr_scr, bias_scr, m_scr, l_scr, acc_scr,
                        *, tq, n_heads, head_dim, n_idx_heads, idx_dim, n_sel, max_chunks):
    i = pl.program_id(0)
    row0 = i * tq
    n_chunks = (row0 + tq + KV_CHUNK - 1) // KV_CHUNK
    idx_scale = (idx_dim ** -0.5) * (n_idx_heads ** -0.5)
    rows = row0 + lax.broadcasted_iota(I32, (tq, 1), 0)

    wi = kiwi_ref[:, idx_dim:idx_dim + n_idx_heads] * idx_scale
    for h in range(n_idx_heads):
        wb_scr[h] = jnp.broadcast_to(wi[:, h:h + 1], (tq, LANES))

    lane_reps = KV_CHUNK // LANES

    def score_chunk(c, carry):
        for sub in range(KV_CHUNK // IDX_SUB):
            col0 = pl.multiple_of(c * KV_CHUNK + sub * IDX_SUB, IDX_SUB)
            ki_c = ki_ref[pl.ds(col0, IDX_SUB), :]
            acc = jnp.zeros((tq, IDX_SUB), F32)
            for h in range(n_idx_heads):
                s = lax.dot_general(qi_ref[:, h * idx_dim:(h + 1) * idx_dim], ki_c, _NT_DIMS,
                                    preferred_element_type=F32)
                wb = wb_scr[h]
                acc = acc + jnp.maximum(s, 0.0) * jnp.concatenate([wb] * (IDX_SUB // LANES), axis=1)
            cols = col0 + lax.broadcasted_iota(I32, (1, IDX_SUB), 1)
            score = jnp.where(cols <= rows, jnp.where(acc == 0.0, 0.0, acc), -jnp.inf)
            blk0 = c * lane_reps + sub * (IDX_SUB // LANES)
            for j in range(IDX_SUB // LANES):
                keys_scr[blk0 + j] = score[:, j * LANES:(j + 1) * LANES]
        return carry

    lax.fori_loop(0, n_chunks, score_chunk, 0)

    def count_ge(cand):
        cand_b = jnp.broadcast_to(cand, (tq, LANES))

        def counter(n):
            def run():
                parts = []
                for rb in range(tq // ROW_SUB):
                    rs = slice(rb * ROW_SUB, (rb + 1) * ROW_SUB)
                    acc = jnp.zeros((ROW_SUB, LANES), I32)
                    for blk in range(n * lane_reps):
                        acc = acc + jnp.where(keys_scr[blk, rs, :] >= cand_b[rs], 1, 0)
                    parts.append(acc)
                return jnp.concatenate(parts, axis=0)
            return run

        acc = lax.switch(n_chunks - 1, [counter(n) for n in range(1, max_chunks + 1)])
        return jnp.sum(acc.astype(F32), axis=1, keepdims=True).astype(I32)

    thr_scr[...] = jnp.broadcast_to(_kth_largest(count_ge, n_sel, (tq, 1)), (tq, LANES))

    m_scr[...] = jnp.full(m_scr.shape, MASK_VALUE, F32)
    l_scr[...] = jnp.zeros(l_scr.shape, F32)
    acc_scr[...] = jnp.zeros(acc_scr.shape, F32)

    att_reps = ATT_CHUNK // LANES

    def attend_chunk(c, carry):
        col0 = pl.multiple_of(c * ATT_CHUNK, ATT_CHUNK)
        cols = col0 + lax.broadcasted_iota(I32, (1, ATT_CHUNK), 1)
        thr_b = jnp.concatenate([thr_scr[...]] * att_reps, axis=1)
        keys = jnp.concatenate([keys_scr[c * att_reps + j] for j in range(att_reps)], axis=1)
        sel = (keys >= thr_b) & (cols <= rows)
        bias_scr[...] = jnp.where(sel, 0.0, MASK_VALUE)
        for h in range(n_heads):
            s = lax.dot_general(q_ref[h], k_ref[h, pl.ds(col0, ATT_CHUNK), :], _NT_DIMS,
                                preferred_element_type=F32) + bias_scr[...]
            m_prev = m_scr[h]
            m_new = jnp.maximum(m_prev, jnp.max(s, axis=1, keepdims=True))
            alpha = jnp.exp2(m_prev - m_new)
            p = jnp.exp2(s - jnp.concatenate([m_new] * att_reps, axis=1))
            l_scr[h] = alpha * l_scr[h] + jnp.sum(p, axis=1, keepdims=True)
            acc_scr[h] = alpha * acc_scr[h] + jnp.dot(p.astype(BF16), v_ref[h, pl.ds(col0, ATT_CHUNK), :],
                                                      preferred_element_type=F32)
            m_scr[h] = m_new
        return carry

    lax.fori_loop(0, (row0 + tq + ATT_CHUNK - 1) // ATT_CHUNK, attend_chunk, 0)

    for h in range(n_heads):
        o_ref[:, h * head_dim:(h + 1) * head_dim] = (acc_scr[h] / l_scr[h]).astype(o_ref.dtype)


def _prompt_attention(q, qi, kiwi, ki, k, v, *, n_heads, head_dim, n_idx_heads, idx_dim):
    t = q.shape[1]
    tq = _row_block(t, 256)
    assert t % KV_CHUNK == 0 and t >= TOPK_MAX * 2 and head_dim == LANES
    n_sel = min(TOPK_MAX, t // 4)
    aw = n_heads * head_dim
    kern = functools.partial(_prompt_attn_kernel, tq=tq, n_heads=n_heads, head_dim=head_dim,
                             n_idx_heads=n_idx_heads, idx_dim=idx_dim, n_sel=n_sel,
                             max_chunks=t // KV_CHUNK)
    return pl.pallas_call(
        kern,
        grid=(t // tq,),
        in_specs=[
            pl.BlockSpec((n_heads, tq, head_dim), lambda i: (0, i, 0)),
            pl.BlockSpec((tq, n_idx_heads * idx_dim), lambda i: (i, 0)),
            pl.BlockSpec((tq, LANES), lambda i: (i, 0)),
            _resident((t, idx_dim), lambda i: (0, 0)),
            _resident((n_heads, t, head_dim), lambda i: (0, 0, 0)),
            _resident((n_heads, t, head_dim), lambda i: (0, 0, 0)),
        ],
        out_specs=pl.BlockSpec((tq, aw), lambda i: (i, 0)),
        out_shape=jax.ShapeDtypeStruct((t, aw), BF16),
        scratch_shapes=[
            pltpu.VMEM((n_idx_heads, tq, LANES), F32),
            pltpu.VMEM((t // LANES, tq, LANES), F32),
            pltpu.VMEM((tq, LANES), F32),
            pltpu.VMEM((tq, ATT_CHUNK), F32),
            pltpu.VMEM((n_heads, tq, LANES), F32),
            pltpu.VMEM((n_heads, tq, LANES), F32),
            pltpu.VMEM((n_heads, tq, head_dim), F32),
        ],
        compiler_params=_params("arbitrary"),
        name="prompt_sparse_attention",
    )(q, qi, kiwi, ki, k, v)


def _softplus(x):
    return jnp.maximum(x, 0.0) + jnp.log1p(jnp.exp(-jnp.abs(x)))


def _split_bf16(x):
    hi = x.astype(BF16)
    lo = (x - hi.astype(F32)).astype(BF16)
    return hi, lo


def _lru_gates(xc, wa_ref, ba_ref, wx_ref, bx_ref, lam_ref):
    n_blocks, blk, _ = wa_ref.shape
    ra, rx = [], []
    for n in range(n_blocks):
        x_hi, x_lo = _split_bf16(xc[:, n * blk:(n + 1) * blk])
        for w_ref, dst in ((wa_ref, ra), (wx_ref, rx)):
            w_hi, w_lo = _split_bf16(w_ref[n])
            dst.append(jnp.dot(x_hi, w_hi, preferred_element_type=F32)
                       + jnp.dot(x_hi, w_lo, preferred_element_type=F32)
                       + jnp.dot(x_lo, w_hi, preferred_element_type=F32))
    r = jax.nn.sigmoid(jnp.concatenate(ra, axis=1) + ba_ref[...])
    gate_i = jax.nn.sigmoid(jnp.concatenate(rx, axis=1) + bx_ref[...])
    log_a = (-LRU_C) * r * _softplus(-lam_ref[...])
    a = jnp.exp(log_a)
    b = jnp.sqrt(jnp.tanh(-log_a) * (1.0 + a * a)) * (gate_i * xc)
    return a, b


SCAN_UNROLL = 8
DMA_UNROLL = 8


def _lru_prompt_kernel(x_ref, y_ref, cw_ref, cb_ref, wa_ref, ba_ref, wx_ref, bx_ref, lam_ref,
                       o_ref, hlast_ref, ext_scr, a_scr, b_scr, h_scr, *, tb, conv_width):
    t = pl.program_id(0)
    pad = SUBLANES
    assert conv_width - 1 <= pad

    @pl.when(t == 0)
    def _():
        ext_scr[0:pad, :] = jnp.zeros((pad, ext_scr.shape[1]), F32)
        hlast_ref[...] = jnp.zeros(hlast_ref.shape, F32)

    ext_scr[pad:pad + tb, :] = x_ref[...]
    conv = None
    for j in range(conv_width):
        term = cw_ref[j:j + 1, :] * ext_scr[pl.ds(pad - (conv_width - 1) + j, tb), :]
        conv = term if conv is None else conv + term
    xc = cb_ref[...] + conv
    a, b = _lru_gates(xc, wa_ref, ba_ref, wx_ref, bx_ref, lam_ref)
    a_scr[...] = a
    b_scr[...] = b

    sub = lax.broadcasted_iota(I32, (SUBLANES, 1), 0)

    def tile(j, h_prev):
        r0 = pl.multiple_of(j * SUBLANES, SUBLANES)
        at = a_scr[pl.ds(r0, SUBLANES), :]
        bt = b_scr[pl.ds(r0, SUBLANES), :]
        for d in (1, 2, 4):
            a_sh = pltpu.roll(at, d, 0)
            b_sh = pltpu.roll(bt, d, 0)
            ok = sub >= d
            bt = jnp.where(ok, at * b_sh + bt, bt)
            at = jnp.where(ok, at * a_sh, at)
        h = at * h_prev + bt
        h_scr[pl.ds(r0, SUBLANES), :] = h
        return jnp.broadcast_to(h[SUBLANES - 1:SUBLANES, :], h.shape)

    h_last = lax.fori_loop(0, tb // SUBLANES, tile, hlast_ref[...], unroll=SCAN_UNROLL)
    hlast_ref[...] = h_last
    ext_scr[0:pad, :] = ext_scr[tb:tb + pad, :]
    o_ref[...] = (h_scr[...] * jax.nn.gelu(y_ref[...])).astype(o_ref.dtype)


def _lru_prompt(x_lru, y_lru, conv_w, conv_b, rg_wa, rg_ba, rg_wx, rg_bx, rg_lambda):
    t, w = x_lru.shape
    tb = _row_block(t, 256)
    cw = conv_w.shape[0]
    vec = lambda a: a.reshape(1, w)
    full = lambda shape: pl.BlockSpec(shape, lambda i: (0,) * len(shape))
    out, h_last = pl.pallas_call(
        functools.partial(_lru_prompt_kernel, tb=tb, conv_width=cw),
        grid=(t // tb,),
        in_specs=[pl.BlockSpec((tb, w), lambda i: (i, 0)), pl.BlockSpec((tb, w), lambda i: (i, 0)),
                  full((cw, w)), full((1, w)), full(rg_wa.shape), full((1, w)), full(rg_wx.shape),
                  full((1, w)), full((1, w))],
        out_specs=[pl.BlockSpec((tb, w), lambda i: (i, 0)), full((SUBLANES, w))],
        out_shape=[jax.ShapeDtypeStruct((t, w), BF16), jax.ShapeDtypeStruct((SUBLANES, w), F32)],
        scratch_shapes=[pltpu.VMEM((tb + SUBLANES, w), F32), pltpu.VMEM((tb, w), F32),
                        pltpu.VMEM((tb, w), F32), pltpu.VMEM((tb, w), F32)],
        compiler_params=_params("arbitrary"),
        name="rglru_prompt",
    )(x_lru, y_lru, conv_w, vec(conv_b), rg_wa, vec(rg_ba), rg_wx, vec(rg_bx), vec(rg_lambda))
    return out, h_last[0]


def _lru_step_kernel(x_ref, y_ref, sc_ref, h0_ref, cw_ref, cb_ref, wa_ref, ba_ref, wx_ref, bx_ref,
                     lam_ref, o_ref, h_ref, *, conv_width):
    conv = None
    for j in range(conv_width):
        src = x_ref[...] if j == conv_width - 1 else sc_ref[j]
        term = cw_ref[j:j + 1, :] * src
        conv = term if conv is None else conv + term
    xc = cb_ref[...] + conv
    a, b = _lru_gates(xc, wa_ref, ba_ref, wx_ref, bx_ref, lam_ref)
    h = b + a * h0_ref[...]
    h_ref[...] = h
    o_ref[...] = (h * jax.nn.gelu(y_ref[...])).astype(o_ref.dtype)


def _lru_step(x_lru, y_lru, state_conv_t, h0, conv_w, conv_b, rg_wa, rg_ba, rg_wx, rg_bx, rg_lambda):
    b, w = x_lru.shape
    vec = lambda a: a.reshape(1, w)
    return pl.pallas_call(
        functools.partial(_lru_step_kernel, conv_width=conv_w.shape[0]),
        out_shape=[jax.ShapeDtypeStruct((b, w), BF16), jax.ShapeDtypeStruct((b, w), F32)],
        compiler_params=pltpu.CompilerParams(vmem_limit_bytes=VMEM_LIMIT_BYTES),
        name="rglru_step",
    )(x_lru, y_lru, state_conv_t, h0, conv_w, vec(conv_b), rg_wa, vec(rg_ba), rg_wx, vec(rg_bx),
      vec(rg_lambda))


def _merge_kernel(x_ref, ao_ref, lo_ref, ga_ref, gb_ref, wba_ref, wbl_ref, wout_ref, g2_ref,
                  x1_ref, h2_ref):
    attn = jnp.dot(ao_ref[...], wba_ref[...], preferred_element_type=F32)
    lru = jnp.dot(lo_ref[...], wbl_ref[...], preferred_element_type=F32)
    merged = jax.nn.sigmoid(ga_ref[...]) * attn + jax.nn.sigmoid(gb_ref[...]) * lru
    x1 = x_ref[...] + jnp.dot(merged.astype(BF16), wout_ref[...], preferred_element_type=F32)
    x1_ref[...] = x1
    h2 = x1 * lax.rsqrt(jnp.mean(x1 * x1, axis=-1, keepdims=True) + EPS) * g2_ref[...]
    h2_ref[...] = h2.astype(h2_ref.dtype)


def _merge(x, attn_o, lru_o, ga, gb, w_ba, w_bl, w_out, norm2_g):
    m, d = x.shape
    tm = _row_block(m, 256)
    row = lambda width: pl.BlockSpec((tm, width), lambda i: (i, 0))
    return pl.pallas_call(
        _merge_kernel,
        grid=(m // tm,),
        in_specs=[row(d), row(attn_o.shape[1]), row(lru_o.shape[1]), row(d), row(d),
                  _resident(w_ba.shape, lambda i: (0, 0)), _resident(w_bl.shape, lambda i: (0, 0)),
                  _resident(w_out.shape, lambda i: (0, 0)), pl.BlockSpec((1, d), lambda i: (0, 0))],
        out_specs=[row(d), row(d)],
        out_shape=[jax.ShapeDtypeStruct((m, d), F32), jax.ShapeDtypeStruct((m, d), BF16)],
        compiler_params=_params("parallel"),
        name="gated_merge",
    )(x, attn_o, lru_o, ga, gb, w_ba, w_bl, w_out, norm2_g.reshape(1, d))


def _ffn_kernel(h_ref, x1_ref, wg_ref, wu_ref, wd_ref, gf_ref, y_ref, acc_scr):
    f = pl.program_id(1)

    @pl.when(f == 0)
    def _():
        acc_scr[...] = jnp.zeros(acc_scr.shape, F32)

    h = h_ref[...]
    gate = jnp.dot(h, wg_ref[...], preferred_element_type=F32)
    up = jnp.dot(h, wu_ref[...], preferred_element_type=F32)
    act = (jax.nn.silu(gate) * up).astype(BF16)
    acc_scr[...] += jnp.dot(act, wd_ref[...], preferred_element_type=F32)

    @pl.when(f == pl.num_programs(1) - 1)
    def _():
        x2 = x1_ref[...] + acc_scr[...]
        y_ref[...] = x2 * lax.rsqrt(jnp.mean(x2 * x2, axis=-1, keepdims=True) + EPS) * gf_ref[...]


def _ffn(h2, x1, w_gate, w_up, w_down, norm_f_g):
    m, d = x1.shape
    ff = w_gate.shape[1]
    tm = _row_block(m, 512)
    tf = 512
    assert ff % tf == 0
    return pl.pallas_call(
        _ffn_kernel,
        grid=(m // tm, ff // tf),
        in_specs=[pl.BlockSpec((tm, d), lambda i, f: (i, 0)), pl.BlockSpec((tm, d), lambda i, f: (i, 0)),
                  pl.BlockSpec((d, tf), lambda i, f: (0, f)), pl.BlockSpec((d, tf), lambda i, f: (0, f)),
                  pl.BlockSpec((tf, d), lambda i, f: (f, 0)), pl.BlockSpec((1, d), lambda i, f: (0, 0))],
        out_specs=pl.BlockSpec((tm, d), lambda i, f: (i, 0)),
        out_shape=jax.ShapeDtypeStruct((m, d), F32),
        scratch_shapes=[pltpu.VMEM((tm, d), F32)],
        compiler_params=_params("parallel", "arbitrary"),
        name="swiglu_ffn",
    )(h2, x1, w_gate, w_up, w_down, norm_f_g.reshape(1, d))


SCORE_PAGES = 16


def _sample_score_kernel(pt_ref, qi_ref, wi_ref, kin_ref, kidx_hbm, keys_ref, keynew_ref,
                         kibuf, sems, *, n_pages, page, n_idx_heads, idx_dim):
    b = pl.program_id(0)
    slot = b % 2
    idx_scale = (idx_dim ** -0.5) * (n_idx_heads ** -0.5)

    def page_copy(seq, sl, p):
        return pltpu.make_async_copy(kidx_hbm.at[pt_ref[seq, p]], kibuf.at[sl, p], sems.at[sl])

    def start_all(seq, sl):
        def body(p, c):
            page_copy(seq, sl, p).start()
            return c

        lax.fori_loop(0, n_pages, body, 0, unroll=DMA_UNROLL)

    @pl.when(b == 0)
    def _():
        start_all(0, 0)

    @pl.when(b + 1 < pl.num_programs(0))
    def _():
        start_all(b + 1, 1 - slot)

    def wait(p, c):
        page_copy(b, slot, p).wait()
        return c

    lax.fori_loop(0, n_pages, wait, 0, unroll=DMA_UNROLL)

    qi = qi_ref[0]
    wcol = wi_ref[0] * idx_scale
    wb = jnp.broadcast_to(wcol, (n_idx_heads, page))

    def score(g, c):
        p0 = pl.multiple_of(g * SCORE_PAGES, SCORE_PAGES)
        pages = kibuf[slot, pl.ds(p0, SCORE_PAGES)]
        kblk = jnp.concatenate([pages[j] for j in range(SCORE_PAGES)], axis=1).astype(BF16)
        s = jnp.dot(qi, kblk, preferred_element_type=F32)
        rows = []
        for j in range(SCORE_PAGES):
            sj = jnp.maximum(s[:, j * page:(j + 1) * page], 0.0) * wb
            rows.append(jnp.sum(sj, axis=0, keepdims=True))
        scores = jnp.concatenate(rows, axis=0)
        keys_ref[0, pl.ds(p0, SCORE_PAGES), :] = jnp.where(scores == 0.0, 0.0, scores)
        return c

    lax.fori_loop(0, n_pages // SCORE_PAGES, score, 0)

    s_new = jnp.sum(qi.astype(F32) * kin_ref[0].astype(BF16).astype(F32), axis=1, keepdims=True)
    i_new = jnp.sum(jnp.maximum(s_new, 0.0) * wcol, axis=0, keepdims=True)
    keynew_ref[0] = jnp.broadcast_to(jnp.where(i_new == 0.0, 0.0, i_new), (1, LANES))


def _sample_pick_kernel(keys_ref, keynew_ref, pt_ref, rows_ref, sel_scr, selnew_scr,
                        *, n_pages, page, n_sel):
    bsz = keys_ref.shape[0]
    key_new = keynew_ref[...][:, :, :1]

    def count_ge(cand):
        cnt = jnp.sum(jnp.where(keys_ref[...] >= cand, 1.0, 0.0), axis=1, keepdims=True)
        cnt = jnp.sum(cnt, axis=2, keepdims=True)
        return (cnt + jnp.where(key_new >= cand, 1.0, 0.0)).astype(I32)

    thr = _kth_largest(count_ge, n_sel, (bsz, 1, 1))
    sel_scr[...] = jnp.where(keys_ref[...] >= thr, 1.0, 0.0)
    selnew_scr[...] = jnp.broadcast_to(jnp.where(key_new >= thr, 1, 0), selnew_scr.shape)

    tri_o = jnp.where(lax.broadcasted_iota(I32, (page, page), 1) <= lax.broadcasted_iota(I32, (page, page), 0),
                      1.0, 0.0).astype(BF16)
    tri_p = jnp.where(lax.broadcasted_iota(I32, (n_pages, n_pages), 1)
                      <= lax.broadcasted_iota(I32, (n_pages, n_pages), 0), 1.0, 0.0).astype(BF16)
    j_row = lax.broadcasted_iota(I32, (1, n_sel), 1).astype(F32)
    p_col = lax.broadcasted_iota(I32, (n_pages, 1), 0).astype(F32)
    reps = n_sel // LANES

    def invert(b, c):
        sel = sel_scr[b]
        cnt = jnp.sum(sel, axis=1, keepdims=True)
        cnt_b = jnp.broadcast_to(cnt, (n_pages, LANES))
        cum_b = jnp.dot(tri_p, cnt_b.astype(BF16), preferred_element_type=F32)
        base_b = cum_b - cnt_b
        tile = lambda a: jnp.concatenate([a] * reps, axis=1)
        page_of = jnp.sum(jnp.where(tile(cum_b) <= j_row, 1.0, 0.0), axis=0, keepdims=True)
        onehot = jnp.where(p_col == page_of, 1.0, 0.0)
        base_j = jnp.sum(onehot * tile(base_b), axis=0, keepdims=True)
        pt_col = pt_ref[b].astype(F32)
        phys_j = jnp.sum(onehot * pt_col, axis=0, keepdims=True)
        incl_t = lax.dot_general(tri_o, sel.astype(BF16), _NT_DIMS, preferred_element_type=F32)
        incl_j = jnp.dot(incl_t.astype(BF16), onehot.astype(BF16), preferred_element_type=F32)
        off_j = jnp.sum(jnp.where(incl_j <= j_row - base_j, 1.0, 0.0), axis=0, keepdims=True)
        row = (phys_j * page + off_j).astype(I32)
        total = jnp.sum(cnt, axis=0, keepdims=True)
        is_new = (selnew_scr[b][:1, :1] > 0) & (j_row == n_sel - 1)
        rows_ref[pl.ds(b, 1), :] = jnp.where(is_new | (j_row >= total), -1, row)
        return c

    lax.fori_loop(0, bsz, invert, 0)


def _sample_select(qi, wi, ki_new, cache_kidx, page_table, *, n_sel):
    bsz, n_idx_heads, idx_dim = qi.shape
    n_pages = page_table.shape[1]
    n_pool, page, _ = cache_kidx.shape
    assert page == LANES and n_pages % SCORE_PAGES == 0 and n_sel % LANES == 0
    assert n_pool * page < 2 ** 24
    kidx_t = jnp.swapaxes(cache_kidx, 1, 2)
    keys, key_new = pl.pallas_call(
        functools.partial(_sample_score_kernel, n_pages=n_pages, page=page, n_idx_heads=n_idx_heads,
                          idx_dim=idx_dim),
        grid_spec=pltpu.PrefetchScalarGridSpec(
            num_scalar_prefetch=1,
            grid=(bsz,),
            in_specs=[pl.BlockSpec((1, n_idx_heads, idx_dim), lambda b, pt: (b, 0, 0)),
                      pl.BlockSpec((1, n_idx_heads, 1), lambda b, pt: (b, 0, 0)),
                      pl.BlockSpec((1, 1, idx_dim), lambda b, pt: (b, 0, 0)),
                      pl.BlockSpec(memory_space=pl.ANY)],
            out_specs=[pl.BlockSpec((1, n_pages, page), lambda b, pt: (b, 0, 0)),
                       pl.BlockSpec((1, 1, LANES), lambda b, pt: (b, 0, 0))],
            scratch_shapes=[pltpu.VMEM((2, n_pages, idx_dim, page), F32), pltpu.SemaphoreType.DMA((2,))],
        ),
        out_shape=[jax.ShapeDtypeStruct((bsz, n_pages, page), F32),
                   jax.ShapeDtypeStruct((bsz, 1, LANES), F32)],
        compiler_params=_params("arbitrary"),
        name="sample_index_scores",
    )(page_table, qi, wi, ki_new, kidx_t)
    return pl.pallas_call(
        functools.partial(_sample_pick_kernel, n_pages=n_pages, page=page, n_sel=n_sel),
        out_shape=jax.ShapeDtypeStruct((bsz, n_sel), I32),
        scratch_shapes=[pltpu.VMEM((bsz, n_pages, page), F32), pltpu.VMEM((bsz, SUBLANES, LANES), I32)],
        compiler_params=pltpu.CompilerParams(vmem_limit_bytes=VMEM_LIMIT_BYTES),
        name="sample_index_pick",
    )(keys, key_new, page_table[:, :, None])


def _sample_attend_kernel(rows_sm, q_ref, kn_ref, vn_ref, ck_hbm, cv_hbm, o_ref,
                          kbuf, vbuf, sems, *, n_sel, n_heads, head_dim):
    b = pl.program_id(0)
    slot = b % 2

    def copies(seq, sl, j):
        row = jnp.maximum(rows_sm[seq, j], 0)
        return (pltpu.make_async_copy(ck_hbm.at[row], kbuf.at[sl, j], sems.at[0, sl]),
                pltpu.make_async_copy(cv_hbm.at[row], vbuf.at[sl, j], sems.at[1, sl]))

    def start_all(seq, sl):
        def body(j, c):
            ck, cv = copies(seq, sl, j)
            ck.start()
            cv.start()
            return c

        lax.fori_loop(0, n_sel, body, 0, unroll=DMA_UNROLL)

    @pl.when(b == 0)
    def _():
        start_all(0, 0)

    @pl.when(b + 1 < pl.num_programs(0))
    def _():
        start_all(b + 1, 1 - slot)

    def wait(j, c):
        ck, cv = copies(b, slot, j)
        ck.wait()
        cv.wait()
        return c

    lax.fori_loop(0, n_sel, wait, 0, unroll=DMA_UNROLL)

    @pl.when(rows_sm[b, n_sel - 1] < 0)
    def _():
        kbuf[slot, n_sel - 1] = kn_ref[0]
        vbuf[slot, n_sel - 1] = vn_ref[0]

    q = q_ref[0].astype(F32)
    s = jnp.sum(kbuf[slot] * q[None], axis=2, keepdims=True)
    p = jnp.exp2(s - jnp.max(s, axis=0, keepdims=True))
    o = jnp.sum(p * vbuf[slot], axis=0) / jnp.sum(p, axis=0)
    o_ref[0] = o.astype(o_ref.dtype)


def _sample_attend(rows, q, k_new, v_new, cache_k, cache_v):
    bsz, n_sel = rows.shape
    n_pool, page, n_heads, head_dim = cache_k.shape
    kern = functools.partial(_sample_attend_kernel, n_sel=n_sel, n_heads=n_heads, head_dim=head_dim)
    per_seq = lambda: pl.BlockSpec((1, n_heads, head_dim), lambda b, r_sm: (b, 0, 0))
    flat = lambda c: c.reshape(n_pool * page, n_heads, head_dim)
    return pl.pallas_call(
        kern,
        grid_spec=pltpu.PrefetchScalarGridSpec(
            num_scalar_prefetch=1,
            grid=(bsz,),
            in_specs=[per_seq(), per_seq(), per_seq(),
                      pl.BlockSpec(memory_space=pl.ANY), pl.BlockSpec(memory_space=pl.ANY)],
            out_specs=per_seq(),
            scratch_shapes=[pltpu.VMEM((2, n_sel, n_heads, head_dim), F32),
                            pltpu.VMEM((2, n_sel, n_heads, head_dim), F32),
                            pltpu.SemaphoreType.DMA((2, 2))],
        ),
        out_shape=jax.ShapeDtypeStruct((bsz, n_heads, head_dim), BF16),
        compiler_params=_params("arbitrary"),
        name="sample_sparse_attention",
    )(rows, q, k_new, v_new, flat(cache_k), flat(cache_v))


def kernel(x_prompt, x_sample, cache_k, cache_v, cache_kidx, state_conv, state_rglru, page_table,
           norm1_g, w_in, conv_w, conv_b, rg_wa, rg_ba, rg_wx, rg_bx, rg_lambda,
           w_branch_attn, w_branch_lru, w_out, norm2_g, w_ffn_gate, w_ffn_up, w_ffn_down, norm_f_g):
    bp, seq, d_model = x_prompt.shape
    bd, dec_seq, _ = x_sample.shape
    _, page, n_heads, head_dim = cache_k.shape
    idx_dim = cache_kidx.shape[2]
    lru_w = conv_w.shape[1]
    conv_width = conv_w.shape[0]
    attn_w = n_heads * head_dim
    n_idx_heads = (w_in.shape[1] - 3 * attn_w - idx_dim - 2 * lru_w - 2 * d_model) // (idx_dim + 1)
    assert bp == 1 and dec_seq == 1
    assert idx_dim + n_idx_heads <= LANES

    sizes = [attn_w, attn_w, attn_w, n_idx_heads * idx_dim, idx_dim + n_idx_heads, lru_w, lru_w,
             d_model, d_model]
    cuts = np.cumsum([0] + sizes)
    w_q, w_k, w_v, w_qi, w_kiwi, w_xl, w_yl, w_ga, w_gb = (
        w_in[:, cuts[n]:cuts[n + 1]].astype(BF16) for n in range(len(sizes)))
    w_kiwi = jnp.pad(w_kiwi, ((0, 0), (0, LANES - w_kiwi.shape[1])))
    tail_w = (w_branch_attn.astype(BF16), w_branch_lru.astype(BF16), w_out.astype(BF16), norm2_g)
    ffn_w = (w_ffn_gate.astype(BF16), w_ffn_up.astype(BF16), w_ffn_down.astype(BF16), norm_f_g)
    lru_p = (conv_w, conv_b, rg_wa, rg_ba, rg_wx, rg_bx, rg_lambda)

    def project(x2d, by_head):
        hd = head_dim if by_head else None
        h = _rmsnorm_bf16(x2d, norm1_g)
        (q,) = _matmul(h, w_q, [BF16], out_scale=head_dim ** -0.5 * LOG2_E, head_dim=hd)
        k, k_bf = _matmul(h, w_k, [F32, BF16], head_dim=hd)
        v, v_bf = _matmul(h, w_v, [F32, BF16], head_dim=hd)
        (qi,) = _matmul(h, w_qi, [BF16])
        (kiwi,) = _matmul(h, w_kiwi, [F32])
        (x_lru,) = _matmul(h, w_xl, [F32])
        (y_lru,) = _matmul(h, w_yl, [F32])
        (ga,) = _matmul(h, w_ga, [F32])
        (gb,) = _matmul(h, w_gb, [F32])
        return q, k, k_bf, v, v_bf, qi, kiwi, x_lru, y_lru, ga, gb

    def finish(x2d, attn_o, lru_o, ga, gb):
        x1, h2 = _merge(x2d, attn_o, lru_o, ga, gb, *tail_w)
        return _ffn(h2, x1, *ffn_w)

    xp = x_prompt.reshape(seq, d_model)
    q, k_p, k_bf, v_p, v_bf, qi, kiwi, x_lru, y_lru, ga, gb = project(xp, by_head=True)
    ki_p = kiwi[:, :idx_dim]
    attn_p = _prompt_attention(q, qi, kiwi, ki_p.astype(BF16), k_bf, v_bf, n_heads=n_heads,
                               head_dim=head_dim, n_idx_heads=n_idx_heads, idx_dim=idx_dim)
    lru_o, h_p = _lru_prompt(x_lru, y_lru, *lru_p)
    y_prompt = finish(xp, attn_p, lru_o, ga, gb).reshape(bp, seq, d_model)
    conv_p = x_lru[seq - (conv_width - 1):].reshape(bp, conv_width - 1, lru_w)

    xs = x_sample.reshape(bd, d_model)
    q, k_s, _, v_s, _, qi, kiwi, x_lru, y_lru, ga, gb = project(xs, by_head=False)
    ki_s = kiwi[:, :idx_dim]
    wi_s = kiwi[:, idx_dim:idx_dim + n_idx_heads]
    past = page_table.shape[1] * page
    n_sel = min(TOPK_MAX, (past + dec_seq) // 4)
    rows = _sample_select(qi.reshape(bd, n_idx_heads, idx_dim), wi_s[:, :, None], ki_s[:, None, :],
                          cache_kidx, page_table, n_sel=n_sel)
    hd = (bd, n_heads, head_dim)
    attn_s = _sample_attend(rows, q.reshape(hd), k_s.reshape(hd), v_s.reshape(hd), cache_k, cache_v)
    lru_o, h_s = _lru_step(x_lru, y_lru, jnp.swapaxes(state_conv, 0, 1), state_rglru, *lru_p)
    y_sample = finish(xs, attn_s.reshape(bd, attn_w), lru_o, ga, gb).reshape(bd, dec_seq, d_model)
    conv_s = jnp.concatenate([state_conv[:, 1:], x_lru[:, None, :]], axis=1)

    return (y_prompt, y_sample,
            k_p.reshape(bp, seq, n_heads, head_dim), v_p.reshape(bp, seq, n_heads, head_dim),
            ki_p.reshape(bp, seq, idx_dim), conv_p, h_p.reshape(bp, lru_w),
            k_s.reshape(bd, dec_seq, n_heads, head_dim), v_s.reshape(bd, dec_seq, n_heads, head_dim),
            ki_s.reshape(bd, dec_seq, idx_dim), conv_s, h_s)
```

```python
import functools

import jax
import jax.numpy as jnp
import numpy as np
from jax import lax
from jax.experimental import pallas as pl
from jax.experimental.pallas import tpu as pltpu

TOPK_MAX = 256
LRU_C = 8.0
EPS = 1e-6
MASK_VALUE = -1e30
LOG2_E = 1.4426950408889634

V7X_VMEM_BYTES = 64 * 1024 * 1024
VMEM_LIMIT_BYTES = 60 * 1024 * 1024
LANES = 128
SUBLANES = 8

BF16 = jnp.bfloat16
F32 = jnp.float32
I32 = jnp.int32
INT32_MIN = -(2 ** 31)

_NT_DIMS = (((1,), (1,)), ((), ()))


def _params(*sem):
    return pltpu.CompilerParams(dimension_semantics=sem, vmem_limit_bytes=VMEM_LIMIT_BYTES)


def _resident(block_shape, index_map):
    return pl.BlockSpec(block_shape, index_map, pipeline_mode=pl.Buffered(1))


def _row_block(m, target):
    tm = min(m, target)
    assert m % tm == 0, (m, tm)
    return tm


def _rmsnorm_kernel(x_ref, g_ref, o_ref):
    x = x_ref[...]
    y = x * lax.rsqrt(jnp.mean(x * x, axis=-1, keepdims=True) + EPS) * g_ref[...]
    o_ref[...] = y.astype(o_ref.dtype)


def _rmsnorm_bf16(x, g):
    m, d = x.shape
    tm = _row_block(m, 512)
    return pl.pallas_call(
        _rmsnorm_kernel,
        grid=(m // tm,),
        in_specs=[pl.BlockSpec((tm, d), lambda i: (i, 0)), pl.BlockSpec((1, d), lambda i: (0, 0))],
        out_specs=pl.BlockSpec((tm, d), lambda i: (i, 0)),
        out_shape=jax.ShapeDtypeStruct((m, d), BF16),
        compiler_params=_params("parallel"),
        name="rmsnorm_bf16",
    )(x, g.reshape(1, d))


def _matmul_kernel(a_ref, w_ref, *o_refs, out_scale):
    acc = jnp.dot(a_ref[...], w_ref[...], preferred_element_type=F32)
    if out_scale is not None:
        acc = acc * out_scale
    for o_ref in o_refs:
        if len(o_ref.shape) == 3:
            hd = o_ref.shape[2]
            for h in range(o_ref.shape[0]):
                o_ref[h] = acc[:, h * hd:(h + 1) * hd].astype(o_ref.dtype)
        else:
            o_ref[...] = acc.astype(o_ref.dtype)


def _matmul(a, w, out_dtypes, out_scale=None, head_dim=None):
    m, k = a.shape
    n = w.shape[1]
    tm = _row_block(m, 1024)
    tn = _row_block(n, 1024)
    by_head = [head_dim is not None and dt == BF16 for dt in out_dtypes]
    assert head_dim is None or tn % head_dim == 0
    outs = pl.pallas_call(
        functools.partial(_matmul_kernel, out_scale=out_scale),
        grid=(n // tn, m // tm),
        in_specs=[pl.BlockSpec((tm, k), lambda j, i: (i, 0)), pl.BlockSpec((k, tn), lambda j, i: (0, j))],
        out_specs=[pl.BlockSpec((tn // head_dim, tm, head_dim), lambda j, i: (j, i, 0)) if hm
                   else pl.BlockSpec((tm, tn), lambda j, i: (i, j)) for hm in by_head],
        out_shape=[jax.ShapeDtypeStruct((n // head_dim, m, head_dim) if hm else (m, n), dt)
                   for hm, dt in zip(by_head, out_dtypes)],
        compiler_params=_params("parallel", "parallel"),
        name="proj_matmul",
    )(a, w)
    return outs


ORDER_NEG_INF = -(2 ** 31) + 2 ** 23 - 1


def _float_at(order):
    bits = order ^ ((order >> 31) & jnp.int32(0x7FFFFFFF))
    return jnp.where(order < ORDER_NEG_INF, -jnp.inf, pltpu.bitcast(bits, F32))


def _kth_largest(count_ge, k, shape):
    def unfinished(state):
        it, _, count = state
        return (it < 32) & jnp.any(count != k)

    def body(state):
        it, order, count = state
        cand = order + (jnp.int32(1) << (jnp.int32(31) - it))
        cand_count = count_ge(_float_at(cand))
        take = cand_count >= k
        return it + 1, jnp.where(take, cand, order), jnp.where(take, cand_count, count)

    init = (jnp.int32(0), jnp.full(shape, INT32_MIN, I32), jnp.full(shape, -1, I32))
    return _float_at(lax.while_loop(unfinished, body, init)[1])


IDX_SUB = 256
KV_CHUNK = 512
ATT_CHUNK = 256
ROW_SUB = 128


def _prompt_attn_kernel(q_ref, qi_ref, kiwi_ref, ki_ref, k_ref, v_ref, o_ref,
                        wb_scr, keys_scr, thr_scr, bias_scr, m_scr, l_scr, acc_scr,
                        *, tq, n_heads, head_dim, n_idx_heads, idx_dim, n_sel, max_chunks):
    i = pl.program_id(0)
    row0 = i * tq
    n_chunks = (row0 + tq + KV_CHUNK - 1) // KV_CHUNK
    idx_scale = (idx_dim ** -0.5) * (n_idx_heads ** -0.5)
    rows = row0 + lax.broadcasted_iota(I32, (tq, 1), 0)

    wi = kiwi_ref[:, idx_dim:idx_dim + n_idx_heads] * idx_scale
    for h in range(n_idx_heads):
        wb_scr[h] = jnp.broadcast_to(wi[:, h:h + 1], (tq, LANES))

    lane_reps = KV_CHUNK // LANES

    def score_chunk(c, carry):
        for sub in range(KV_CHUNK // IDX_SUB):
            col0 = pl.multiple_of(c * KV_CHUNK + sub * IDX_SUB, IDX_SUB)
            ki_c = ki_ref[pl.ds(col0, IDX_SUB), :]
            acc = jnp.zeros((tq, IDX_SUB), F32)
            for h in range(n_idx_heads):
                s = lax.dot_general(qi_ref[:, h * idx_dim:(h + 1) * idx_dim], ki_c, _NT_DIMS,
                                    preferred_element_type=F32)
                wb = wb_scr[h]
                acc = acc + jnp.maximum(s, 0.0) * jnp.concatenate([wb] * (IDX_SUB // LANES), axis=1)
            cols = col0 + lax.broadcasted_iota(I32, (1, IDX_SUB), 1)
            score = jnp.where(cols <= rows, jnp.where(acc == 0.0, 0.0, acc), -jnp.inf)
            blk0 = c * lane_reps + sub * (IDX_SUB // LANES)
            for j in range(IDX_SUB // LANES):
                keys_scr[blk0 + j] = score[:, j * LANES:(j + 1) * LANES]
        return carry

    lax.fori_loop(0, n_chunks, score_chunk, 0)

    def count_ge(cand):
        cand_b = jnp.broadcast_to(cand, (tq, LANES))

        def counter(n):
            def run():
                parts = []
                for rb in range(tq // ROW_SUB):
                    rs = slice(rb * ROW_SUB, (rb + 1) * ROW_SUB)
                    acc = jnp.zeros((ROW_SUB, LANES), I32)
                    for blk in range(n * lane_reps):
                        acc = acc + jnp.where(keys_scr[blk, rs, :] >= cand_b[rs], 1, 0)
                    parts.append(acc)
                return jnp.concatenate(parts, axis=0)
            return run

        acc = lax.switch(n_chunks - 1, [counter(n) for n in range(1, max_chunks + 1)])
        return jnp.sum(acc.astype(F32), axis=1, keepdims=True).astype(I32)

    thr_scr[...] = jnp.broadcast_to(_kth_largest(count_ge, n_sel, (tq, 1)), (tq, LANES))

    m_scr[...] = jnp.full(m_scr.shape, MASK_VALUE, F32)
    l_scr[...] = jnp.zeros(l_scr.shape, F32)
    acc_scr[...] = jnp.zeros(acc_scr.shape, F32)

    att_reps = ATT_CHUNK // LANES

    def attend_chunk(c, carry):
        col0 = pl.multiple_of(c * ATT_CHUNK, ATT_CHUNK)
        cols = col0 + lax.broadcasted_iota(I32, (1, ATT_CHUNK), 1)
        thr_b = jnp.concatenate([thr_scr[...]] * att_reps, axis=1)
        keys = jnp.concatenate([keys_scr[c * att_reps + j] for j in range(att_reps)], axis=1)
        sel = (keys >= thr_b) & (cols <= rows)
        bias_scr[...] = jnp.where(sel, 0.0, MASK_VALUE)
        for h in range(n_heads):
            s = lax.dot_general(q_ref[h], k_ref[h, pl.ds(col0, ATT_CHUNK), :], _NT_DIMS,
                                preferred_element_type=F32) + bias_scr[...]
            m_prev = m_scr[h]
            m_new = jnp.maximum(m_prev, jnp.max(s, axis=1, keepdims=True))
            alpha = jnp.exp2(m_prev - m_new)
            p = jnp.exp2(s - jnp.concatenate([m_new] * att_reps, axis=1))
            l_scr[h] = alpha * l_scr[h] + jnp.sum(p, axis=1, keepdims=True)
            acc_scr[h] = alpha * acc_scr[h] + jnp.dot(p.astype(BF16), v_ref[h, pl.ds(col0, ATT_CHUNK), :],
                                                      preferred_element_type=F32)
            m_scr[h] = m_new
        return carry

    lax.fori_loop(0, (row0 + tq + ATT_CHUNK - 1) // ATT_CHUNK, attend_chunk, 0)

    for h in range(n_heads):
        o_ref[:, h * head_dim:(h + 1) * head_dim] = (acc_scr[h] / l_scr[h]).astype(o_ref.dtype)


def _prompt_attention(q, qi, kiwi, ki, k, v, *, n_heads, head_dim, n_idx_heads, idx_dim):
    t = q.shape[1]
    tq = _row_block(t, 256)
    assert t % KV_CHUNK == 0 and t >= TOPK_MAX * 2 and head_dim == LANES
    n_sel = min(TOPK_MAX, t // 4)
    aw = n_heads * head_dim
    kern = functools.partial(_prompt_attn_kernel, tq=tq, n_heads=n_heads, head_dim=head_dim,
                             n_idx_heads=n_idx_heads, idx_dim=idx_dim, n_sel=n_sel,
                             max_chunks=t // KV_CHUNK)
    return pl.pallas_call(
        kern,
        grid=(t // tq,),
        in_specs=[
            pl.BlockSpec((n_heads, tq, head_dim), lambda i: (0, i, 0)),
            pl.BlockSpec((tq, n_idx_heads * idx_dim), lambda i: (i, 0)),
            pl.BlockSpec((tq, LANES), lambda i: (i, 0)),
            _resident((t, idx_dim), lambda i: (0, 0)),
            _resident((n_heads, t, head_dim), lambda i: (0, 0, 0)),
            _resident((n_heads, t, head_dim), lambda i: (0, 0, 0)),
        ],
        out_specs=pl.BlockSpec((tq, aw), lambda i: (i, 0)),
        out_shape=jax.ShapeDtypeStruct((t, aw), BF16),
        scratch_shapes=[
            pltpu.VMEM((n_idx_heads, tq, LANES), F32),
            pltpu.VMEM((t // LANES, tq, LANES), F32),
            pltpu.VMEM((tq, LANES), F32),
            pltpu.VMEM((tq, ATT_CHUNK), F32),
            pltpu.VMEM((n_heads, tq, LANES), F32),
            pltpu.VMEM((n_heads, tq, LANES), F32),
            pltpu.VMEM((n_heads, tq, head_dim), F32),
        ],
        compiler_params=_params("arbitrary"),
        name="prompt_sparse_attention",
    )(q, qi, kiwi, ki, k, v)


def _softplus(x):
    return jnp.maximum(x, 0.0) + jnp.log1p(jnp.exp(-jnp.abs(x)))


def _split_bf16(x):
    hi = x.astype(BF16)
    lo = (x - hi.astype(F32)).astype(BF16)
    return hi, lo


def _lru_gates(xc, wa_ref, ba_ref, wx_ref, bx_ref, lam_ref):
    n_blocks, blk, _ = wa_ref.shape
    ra, rx = [], []
    for n in range(n_blocks):
        x_hi, x_lo = _split_bf16(xc[:, n * blk:(n + 1) * blk])
        for w_ref, dst in ((wa_ref, ra), (wx_ref, rx)):
            w_hi, w_lo = _split_bf16(w_ref[n])
            dst.append(jnp.dot(x_hi, w_hi, preferred_element_type=F32)
                       + jnp.dot(x_hi, w_lo, preferred_element_type=F32)
                       + jnp.dot(x_lo, w_hi, preferred_element_type=F32))
    r = jax.nn.sigmoid(jnp.concatenate(ra, axis=1) + ba_ref[...])
    gate_i = jax.nn.sigmoid(jnp.concatenate(rx, axis=1) + bx_ref[...])
    log_a = (-LRU_C) * r * _softplus(-lam_ref[...])
    a = jnp.exp(log_a)
    b = jnp.sqrt(jnp.tanh(-log_a) * (1.0 + a * a)) * (gate_i * xc)
    return a, b


SCAN_UNROLL = 8
DMA_UNROLL = 8


def _lru_prompt_kernel(x_ref, y_ref, cw_ref, cb_ref, wa_ref, ba_ref, wx_ref, bx_ref, lam_ref,
                       o_ref, hlast_ref, ext_scr, a_scr, b_scr, h_scr, *, tb, conv_width):
    t = pl.program_id(0)
    pad = SUBLANES
    assert conv_width - 1 <= pad

    @pl.when(t == 0)
    def _():
        ext_scr[0:pad, :] = jnp.zeros((pad, ext_scr.shape[1]), F32)
        hlast_ref[...] = jnp.zeros(hlast_ref.shape, F32)

    ext_scr[pad:pad + tb, :] = x_ref[...]
    conv = None
    for j in range(conv_width):
        term = cw_ref[j:j + 1, :] * ext_scr[pl.ds(pad - (conv_width - 1) + j, tb), :]
        conv = term if conv is None else conv + term
    xc = cb_ref[...] + conv
    a, b = _lru_gates(xc, wa_ref, ba_ref, wx_ref, bx_ref, lam_ref)
    a_scr[...] = a
    b_scr[...] = b

    sub = lax.broadcasted_iota(I32, (SUBLANES, 1), 0)

    def tile(j, h_prev):
        r0 = pl.multiple_of(j * SUBLANES, SUBLANES)
        at = a_scr[pl.ds(r0, SUBLANES), :]
        bt = b_scr[pl.ds(r0, SUBLANES), :]
        for d in (1, 2, 4):
            a_sh = pltpu.roll(at, d, 0)
            b_sh = pltpu.roll(bt, d, 0)
            ok = sub >= d
            bt = jnp.where(ok, at * b_sh + bt, bt)
            at = jnp.where(ok, at * a_sh, at)
        h = at * h_prev + bt
        h_scr[pl.ds(r0, SUBLANES), :] = h
        return jnp.broadcast_to(h[SUBLANES - 1:SUBLANES, :], h.shape)

    h_last = lax.fori_loop(0, tb // SUBLANES, tile, hlast_ref[...], unroll=SCAN_UNROLL)
    hlast_ref[...] = h_last
    ext_scr[0:pad, :] = ext_scr[tb:tb + pad, :]
    o_ref[...] = (h_scr[...] * jax.nn.gelu(y_ref[...])).astype(o_ref.dtype)


def _lru_prompt(x_lru, y_lru, conv_w, conv_b, rg_wa, rg_ba, rg_wx, rg_bx, rg_lambda):
    t, w = x_lru.shape
    tb = _row_block(t, 256)
    cw = conv_w.shape[0]
    vec = lambda a: a.reshape(1, w)
    full = lambda shape: pl.BlockSpec(shape, lambda i: (0,) * len(shape))
    out, h_last = pl.pallas_call(
        functools.partial(_lru_prompt_kernel, tb=tb, conv_width=cw),
        grid=(t // tb,),
        in_specs=[pl.BlockSpec((tb, w), lambda i: (i, 0)), pl.BlockSpec((tb, w), lambda i: (i, 0)),
                  full((cw, w)), full((1, w)), full(rg_wa.shape), full((1, w)), full(rg_wx.shape),
                  full((1, w)), full((1, w))],
        out_specs=[pl.BlockSpec((tb, w), lambda i: (i, 0)), full((SUBLANES, w))],
        out_shape=[jax.ShapeDtypeStruct((t, w), BF16), jax.ShapeDtypeStruct((SUBLANES, w), F32)],
        scratch_shapes=[pltpu.VMEM((tb + SUBLANES, w), F32), pltpu.VMEM((tb, w), F32),
                        pltpu.VMEM((tb, w), F32), pltpu.VMEM((tb, w), F32)],
        compiler_params=_params("arbitrary"),
        name="rglru_prompt",
    )(x_lru, y_lru, conv_w, vec(conv_b), rg_wa, vec(rg_ba), rg_wx, vec(rg_bx), vec(rg_lambda))
    return out, h_last[0]


def _lru_step_kernel(x_ref, y_ref, sc_ref, h0_ref, cw_ref, cb_ref, wa_ref, ba_ref, wx_ref, bx_ref,
                     lam_ref, o_ref, h_ref, *, conv_width):
    conv = None
    for j in range(conv_width):
        src = x_ref[...] if j == conv_width - 1 else sc_ref[j]
        term = cw_ref[j:j + 1, :] * src
        conv = term if conv is None else conv + term
    xc = cb_ref[...] + conv
    a, b = _lru_gates(xc, wa_ref, ba_ref, wx_ref, bx_ref, lam_ref)
    h = b + a * h0_ref[...]
    h_ref[...] = h
    o_ref[...] = (h * jax.nn.gelu(y_ref[...])).astype(o_ref.dtype)


def _lru_step(x_lru, y_lru, state_conv_t, h0, conv_w, conv_b, rg_wa, rg_ba, rg_wx, rg_bx, rg_lambda):
    b, w = x_lru.shape
    vec = lambda a: a.reshape(1, w)
    return pl.pallas_call(
        functools.partial(_lru_step_kernel, conv_width=conv_w.shape[0]),
        out_shape=[jax.ShapeDtypeStruct((b, w), BF16), jax.ShapeDtypeStruct((b, w), F32)],
        compiler_params=pltpu.CompilerParams(vmem_limit_bytes=VMEM_LIMIT_BYTES),
        name="rglru_step",
    )(x_lru, y_lru, state_conv_t, h0, conv_w, vec(conv_b), rg_wa, vec(rg_ba), rg_wx, vec(rg_bx),
      vec(rg_lambda))


def _merge_kernel(x_ref, ao_ref, lo_ref, ga_ref, gb_ref, wba_ref, wbl_ref, wout_ref, g2_ref,
                  x1_ref, h2_ref):
    attn = jnp.dot(ao_ref[...], wba_ref[...], preferred_element_type=F32)
    lru = jnp.dot(lo_ref[...], wbl_ref[...], preferred_element_type=F32)
    merged = jax.nn.sigmoid(ga_ref[...]) * attn + jax.nn.sigmoid(gb_ref[...]) * lru
    x1 = x_ref[...] + jnp.dot(merged.astype(BF16), wout_ref[...], preferred_element_type=F32)
    x1_ref[...] = x1
    h2 = x1 * lax.rsqrt(jnp.mean(x1 * x1, axis=-1, keepdims=True) + EPS) * g2_ref[...]
    h2_ref[...] = h2.astype(h2_ref.dtype)


def _merge(x, attn_o, lru_o, ga, gb, w_ba, w_bl, w_out, norm2_g):
    m, d = x.shape
    tm = _row_block(m, 256)
    row = lambda width: pl.BlockSpec((tm, width), lambda i: (i, 0))
    return pl.pallas_call(
        _merge_kernel,
        grid=(m // tm,),
        in_specs=[row(d), row(attn_o.shape[1]), row(lru_o.shape[1]), row(d), row(d),
                  _resident(w_ba.shape, lambda i: (0, 0)), _resident(w_bl.shape, lambda i: (0, 0)),
                  _resident(w_out.shape, lambda i: (0, 0)), pl.BlockSpec((1, d), lambda i: (0, 0))],
        out_specs=[row(d), row(d)],
        out_shape=[jax.ShapeDtypeStruct((m, d), F32), jax.ShapeDtypeStruct((m, d), BF16)],
        compiler_params=_params("parallel"),
        name="gated_merge",
    )(x, attn_o, lru_o, ga, gb, w_ba, w_bl, w_out, norm2_g.reshape(1, d))


def _ffn_kernel(h_ref, x1_ref, wg_ref, wu_ref, wd_ref, gf_ref, y_ref, acc_scr):
    f = pl.program_id(1)

    @pl.when(f == 0)
    def _():
        acc_scr[...] = jnp.zeros(acc_scr.shape, F32)

    h = h_ref[...]
    gate = jnp.dot(h, wg_ref[...], preferred_element_type=F32)
    up = jnp.dot(h, wu_ref[...], preferred_element_type=F32)
    act = (jax.nn.silu(gate) * up).astype(BF16)
    acc_scr[...] += jnp.dot(act, wd_ref[...], preferred_element_type=F32)

    @pl.when(f == pl.num_programs(1) - 1)
    def _():
        x2 = x1_ref[...] + acc_scr[...]
        y_ref[...] = x2 * lax.rsqrt(jnp.mean(x2 * x2, axis=-1, keepdims=True) + EPS) * gf_ref[...]


def _ffn(h2, x1, w_gate, w_up, w_down, norm_f_g):
    m, d = x1.shape
    ff = w_gate.shape[1]
    tm = _row_block(m, 512)
    tf = 512
    assert ff % tf == 0
    return pl.pallas_call(
        _ffn_kernel,
        grid=(m // tm, ff // tf),
        in_specs=[pl.BlockSpec((tm, d), lambda i, f: (i, 0)), pl.BlockSpec((tm, d), lambda i, f: (i, 0)),
                  pl.BlockSpec((d, tf), lambda i, f: (0, f)), pl.BlockSpec((d, tf), lambda i, f: (0, f)),
                  pl.BlockSpec((tf, d), lambda i, f: (f, 0)), pl.BlockSpec((1, d), lambda i, f: (0, 0))],
        out_specs=pl.BlockSpec((tm, d), lambda i, f: (i, 0)),
        out_shape=jax.ShapeDtypeStruct((m, d), F32),
        scratch_shapes=[pltpu.VMEM((tm, d), F32)],
        compiler_params=_params("parallel", "arbitrary"),
        name="swiglu_ffn",
    )(h2, x1, w_gate, w_up, w_down, norm_f_g.reshape(1, d))


SCORE_PAGES = 16


def _sample_score_kernel(pt_ref, qi_ref, wi_ref, kin_ref, kidx_hbm, keys_ref, keynew_ref,
                         kibuf, sems, *, n_pages, page, n_idx_heads, idx_dim):
    b = pl.program_id(0)
    slot = b % 2
    idx_scale = (idx_dim ** -0.5) * (n_idx_heads ** -0.5)

    def page_copy(seq, sl, p):
        return pltpu.make_async_copy(kidx_hbm.at[pt_ref[seq, p]], kibuf.at[sl, p], sems.at[sl])

    def start_all(seq, sl):
        def body(g, c):
            for u in range(DMA_UNROLL):
                page_copy(seq, sl, g * DMA_UNROLL + u).start(priority=u % 2)
            return c

        lax.fori_loop(0, n_pages // DMA_UNROLL, body, 0)

    @pl.when(b == 0)
    def _():
        start_all(0, 0)

    @pl.when(b + 1 < pl.num_programs(0))
    def _():
        start_all(b + 1, 1 - slot)

    def wait(p, c):
        page_copy(b, slot, p).wait()
        return c

    lax.fori_loop(0, n_pages, wait, 0, unroll=DMA_UNROLL)

    qi = qi_ref[0]
    wcol = wi_ref[0] * idx_scale
    wb = jnp.broadcast_to(wcol, (n_idx_heads, page))

    def score(g, c):
        p0 = pl.multiple_of(g * SCORE_PAGES, SCORE_PAGES)
        pages = kibuf[slot, pl.ds(p0, SCORE_PAGES)]
        kblk = jnp.concatenate([pages[j] for j in range(SCORE_PAGES)], axis=1).astype(BF16)
        s = jnp.dot(qi, kblk, preferred_element_type=F32)
        rows = []
        for j in range(SCORE_PAGES):
            sj = jnp.maximum(s[:, j * page:(j + 1) * page], 0.0) * wb
            rows.append(jnp.sum(sj, axis=0, keepdims=True))
        scores = jnp.concatenate(rows, axis=0)
        keys_ref[0, pl.ds(p0, SCORE_PAGES), :] = jnp.where(scores == 0.0, 0.0, scores)
        return c

    lax.fori_loop(0, n_pages // SCORE_PAGES, score, 0)

    s_new = jnp.sum(qi.astype(F32) * kin_ref[0].astype(BF16).astype(F32), axis=1, keepdims=True)
    i_new = jnp.sum(jnp.maximum(s_new, 0.0) * wcol, axis=0, keepdims=True)
    keynew_ref[0] = jnp.broadcast_to(jnp.where(i_new == 0.0, 0.0, i_new), (1, LANES))


def _sample_pick_kernel(keys_ref, keynew_ref, pt_ref, rows_ref, sel_scr, selnew_scr,
                        *, n_pages, page, n_sel):
    bsz = keys_ref.shape[0]
    key_new = keynew_ref[...][:, :, :1]

    def count_ge(cand):
        cnt = jnp.sum(jnp.where(keys_ref[...] >= cand, 1.0, 0.0), axis=1, keepdims=True)
        cnt = jnp.sum(cnt, axis=2, keepdims=True)
        return (cnt + jnp.where(key_new >= cand, 1.0, 0.0)).astype(I32)

    thr = _kth_largest(count_ge, n_sel, (bsz, 1, 1))
    sel_scr[...] = jnp.where(keys_ref[...] >= thr, 1.0, 0.0)
    selnew_scr[...] = jnp.broadcast_to(jnp.where(key_new >= thr, 1, 0), selnew_scr.shape)

    tri_o = jnp.where(lax.broadcasted_iota(I32, (page, page), 1) <= lax.broadcasted_iota(I32, (page, page), 0),
                      1.0, 0.0).astype(BF16)
    tri_p = jnp.where(lax.broadcasted_iota(I32, (n_pages, n_pages), 1)
                      <= lax.broadcasted_iota(I32, (n_pages, n_pages), 0), 1.0, 0.0).astype(BF16)
    j_row = lax.broadcasted_iota(I32, (1, n_sel), 1).astype(F32)
    p_col = lax.broadcasted_iota(I32, (n_pages, 1), 0).astype(F32)
    reps = n_sel // LANES

    def invert(b, c):
        sel = sel_scr[b]
        cnt = jnp.sum(sel, axis=1, keepdims=True)
        cnt_b = jnp.broadcast_to(cnt, (n_pages, LANES))
        cum_b = jnp.dot(tri_p, cnt_b.astype(BF16), preferred_element_type=F32)
        base_b = cum_b - cnt_b
        tile = lambda a: jnp.concatenate([a] * reps, axis=1)
        page_of = jnp.sum(jnp.where(tile(cum_b) <= j_row, 1.0, 0.0), axis=0, keepdims=True)
        onehot = jnp.where(p_col == page_of, 1.0, 0.0)
        base_j = jnp.sum(onehot * tile(base_b), axis=0, keepdims=True)
        pt_col = pt_ref[b].astype(F32)
        phys_j = jnp.sum(onehot * pt_col, axis=0, keepdims=True)
        incl_t = lax.dot_general(tri_o, sel.astype(BF16), _NT_DIMS, preferred_element_type=F32)
        incl_j = jnp.dot(incl_t.astype(BF16), onehot.astype(BF16), preferred_element_type=F32)
        off_j = jnp.sum(jnp.where(incl_j <= j_row - base_j, 1.0, 0.0), axis=0, keepdims=True)
        row = (phys_j * page + off_j).astype(I32)
        total = jnp.sum(cnt, axis=0, keepdims=True)
        is_new = (selnew_scr[b][:1, :1] > 0) & (j_row == n_sel - 1)
        rows_ref[pl.ds(b, 1), :] = jnp.where(is_new | (j_row >= total), -1, row)
        return c

    lax.fori_loop(0, bsz, invert, 0)


def _sample_select(qi, wi, ki_new, cache_kidx, page_table, *, n_sel):
    bsz, n_idx_heads, idx_dim = qi.shape
    n_pages = page_table.shape[1]
    n_pool, page, _ = cache_kidx.shape
    assert page == LANES and n_pages % SCORE_PAGES == 0 and n_sel % LANES == 0
    assert n_pool * page < 2 ** 24
    kidx_t = jnp.swapaxes(cache_kidx, 1, 2)
    keys, key_new = pl.pallas_call(
        functools.partial(_sample_score_kernel, n_pages=n_pages, page=page, n_idx_heads=n_idx_heads,
                          idx_dim=idx_dim),
        grid_spec=pltpu.PrefetchScalarGridSpec(
            num_scalar_prefetch=1,
            grid=(bsz,),
            in_specs=[pl.BlockSpec((1, n_idx_heads, idx_dim), lambda b, pt: (b, 0, 0)),
                      pl.BlockSpec((1, n_idx_heads, 1), lambda b, pt: (b, 0, 0)),
                      pl.BlockSpec((1, 1, idx_dim), lambda b, pt: (b, 0, 0)),
                      pl.BlockSpec(memory_space=pl.ANY)],
            out_specs=[pl.BlockSpec((1, n_pages, page), lambda b, pt: (b, 0, 0)),
                       pl.BlockSpec((1, 1, LANES), lambda b, pt: (b, 0, 0))],
            scratch_shapes=[pltpu.VMEM((2, n_pages, idx_dim, page), F32), pltpu.SemaphoreType.DMA((2,))],
        ),
        out_shape=[jax.ShapeDtypeStruct((bsz, n_pages, page), F32),
                   jax.ShapeDtypeStruct((bsz, 1, LANES), F32)],
        compiler_params=_params("arbitrary"),
        name="sample_index_scores",
    )(page_table, qi, wi, ki_new, kidx_t)
    return pl.pallas_call(
        functools.partial(_sample_pick_kernel, n_pages=n_pages, page=page, n_sel=n_sel),
        out_shape=jax.ShapeDtypeStruct((bsz, n_sel), I32),
        scratch_shapes=[pltpu.VMEM((bsz, n_pages, page), F32), pltpu.VMEM((bsz, SUBLANES, LANES), I32)],
        compiler_params=pltpu.CompilerParams(vmem_limit_bytes=VMEM_LIMIT_BYTES),
        name="sample_index_pick",
    )(keys, key_new, page_table[:, :, None])


def _sample_attend_kernel(rows_sm, q_ref, kn_ref, vn_ref, ck_hbm, cv_hbm, o_ref,
                          kbuf, vbuf, sems, *, n_sel, n_heads, head_dim):
    b = pl.program_id(0)
    slot = b % 2

    def copies(seq, sl, j):
        row = jnp.maximum(rows_sm[seq, j], 0)
        return (pltpu.make_async_copy(ck_hbm.at[row], kbuf.at[sl, j], sems.at[0, sl]),
                pltpu.make_async_copy(cv_hbm.at[row], vbuf.at[sl, j], sems.at[1, sl]))

    def start_all(seq, sl):
        def body(j, c):
            ck, cv = copies(seq, sl, j)
            ck.start(priority=0)
            cv.start(priority=1)
            return c

        lax.fori_loop(0, n_sel, body, 0, unroll=DMA_UNROLL)

    @pl.when(b == 0)
    def _():
        start_all(0, 0)

    @pl.when(b + 1 < pl.num_programs(0))
    def _():
        start_all(b + 1, 1 - slot)

    def wait(j, c):
        ck, cv = copies(b, slot, j)
        ck.wait()
        cv.wait()
        return c

    lax.fori_loop(0, n_sel, wait, 0, unroll=DMA_UNROLL)

    @pl.when(rows_sm[b, n_sel - 1] < 0)
    def _():
        kbuf[slot, n_sel - 1] = kn_ref[0]
        vbuf[slot, n_sel - 1] = vn_ref[0]

    q = q_ref[0].astype(F32)
    s = jnp.sum(kbuf[slot] * q[None], axis=2, keepdims=True)
    p = jnp.exp2(s - jnp.max(s, axis=0, keepdims=True))
    o = jnp.sum(p * vbuf[slot], axis=0) / jnp.sum(p, axis=0)
    o_ref[0] = o.astype(o_ref.dtype)


def _sample_attend(rows, q, k_new, v_new, cache_k, cache_v):
    bsz, n_sel = rows.shape
    n_pool, page, n_heads, head_dim = cache_k.shape
    kern = functools.partial(_sample_attend_kernel, n_sel=n_sel, n_heads=n_heads, head_dim=head_dim)
    per_seq = lambda: pl.BlockSpec((1, n_heads, head_dim), lambda b, r_sm: (b, 0, 0))
    flat = lambda c: c.reshape(n_pool * page, n_heads, head_dim)
    return pl.pallas_call(
        kern,
        grid_spec=pltpu.PrefetchScalarGridSpec(
            num_scalar_prefetch=1,
            grid=(bsz,),
            in_specs=[per_seq(), per_seq(), per_seq(),
                      pl.BlockSpec(memory_space=pl.ANY), pl.BlockSpec(memory_space=pl.ANY)],
            out_specs=per_seq(),
            scratch_shapes=[pltpu.VMEM((2, n_sel, n_heads, head_dim), F32),
                            pltpu.VMEM((2, n_sel, n_heads, head_dim), F32),
                            pltpu.SemaphoreType.DMA((2, 2))],
        ),
        out_shape=jax.ShapeDtypeStruct((bsz, n_heads, head_dim), BF16),
        compiler_params=_params("arbitrary"),
        name="sample_sparse_attention",
    )(rows, q, k_new, v_new, flat(cache_k), flat(cache_v))


def kernel(x_prompt, x_sample, cache_k, cache_v, cache_kidx, state_conv, state_rglru, page_table,
           norm1_g, w_in, conv_w, conv_b, rg_wa, rg_ba, rg_wx, rg_bx, rg_lambda,
           w_branch_attn, w_branch_lru, w_out, norm2_g, w_ffn_gate, w_ffn_up, w_ffn_down, norm_f_g):
    bp, seq, d_model = x_prompt.shape
    bd, dec_seq, _ = x_sample.shape
    _, page, n_heads, head_dim = cache_k.shape
    idx_dim = cache_kidx.shape[2]
    lru_w = conv_w.shape[1]
    conv_width = conv_w.shape[0]
    attn_w = n_heads * head_dim
    n_idx_heads = (w_in.shape[1] - 3 * attn_w - idx_dim - 2 * lru_w - 2 * d_model) // (idx_dim + 1)
    assert bp == 1 and dec_seq == 1
    assert idx_dim + n_idx_heads <= LANES

    sizes = [attn_w, attn_w, attn_w, n_idx_heads * idx_dim, idx_dim + n_idx_heads, lru_w, lru_w,
             d_model, d_model]
    cuts = np.cumsum([0] + sizes)
    w_q, w_k, w_v, w_qi, w_kiwi, w_xl, w_yl, w_ga, w_gb = (
        w_in[:, cuts[n]:cuts[n + 1]].astype(BF16) for n in range(len(sizes)))
    w_kiwi = jnp.pad(w_kiwi, ((0, 0), (0, LANES - w_kiwi.shape[1])))
    tail_w = (w_branch_attn.astype(BF16), w_branch_lru.astype(BF16), w_out.astype(BF16), norm2_g)
    ffn_w = (w_ffn_gate.astype(BF16), w_ffn_up.astype(BF16), w_ffn_down.astype(BF16), norm_f_g)
    lru_p = (conv_w, conv_b, rg_wa, rg_ba, rg_wx, rg_bx, rg_lambda)

    def project(x2d, by_head):
        hd = head_dim if by_head else None
        h = _rmsnorm_bf16(x2d, norm1_g)
        (q,) = _matmul(h, w_q, [BF16], out_scale=head_dim ** -0.5 * LOG2_E, head_dim=hd)
        k, k_bf = _matmul(h, w_k, [F32, BF16], head_dim=hd)
        v, v_bf = _matmul(h, w_v, [F32, BF16], head_dim=hd)
        (qi,) = _matmul(h, w_qi, [BF16])
        (kiwi,) = _matmul(h, w_kiwi, [F32])
        (x_lru,) = _matmul(h, w_xl, [F32])
        (y_lru,) = _matmul(h, w_yl, [F32])
        (ga,) = _matmul(h, w_ga, [F32])
        (gb,) = _matmul(h, w_gb, [F32])
        return q, k, k_bf, v, v_bf, qi, kiwi, x_lru, y_lru, ga, gb

    def finish(x2d, attn_o, lru_o, ga, gb):
        x1, h2 = _merge(x2d, attn_o, lru_o, ga, gb, *tail_w)
        return _ffn(h2, x1, *ffn_w)

    xp = x_prompt.reshape(seq, d_model)
    q, k_p, k_bf, v_p, v_bf, qi, kiwi, x_lru, y_lru, ga, gb = project(xp, by_head=True)
    ki_p = kiwi[:, :idx_dim]
    attn_p = _prompt_attention(q, qi, kiwi, ki_p.astype(BF16), k_bf, v_bf, n_heads=n_heads,
                               head_dim=head_dim, n_idx_heads=n_idx_heads, idx_dim=idx_dim)
    lru_o, h_p = _lru_prompt(x_lru, y_lru, *lru_p)
    y_prompt = finish(xp, attn_p, lru_o, ga, gb).reshape(bp, seq, d_model)
    conv_p = x_lru[seq - (conv_width - 1):].reshape(bp, conv_width - 1, lru_w)

    xs = x_sample.reshape(bd, d_model)
    q, k_s, _, v_s, _, qi, kiwi, x_lru, y_lru, ga, gb = project(xs, by_head=False)
    ki_s = kiwi[:, :idx_dim]
    wi_s = kiwi[:, idx_dim:idx_dim + n_idx_heads]
    past = page_table.shape[1] * page
    n_sel = min(TOPK_MAX, (past + dec_seq) // 4)
    rows = _sample_select(qi.reshape(bd, n_idx_heads, idx_dim), wi_s[:, :, None], ki_s[:, None, :],
                          cache_kidx, page_table, n_sel=n_sel)
    hd = (bd, n_heads, head_dim)
    attn_s = _sample_attend(rows, q.reshape(hd), k_s.reshape(hd), v_s.reshape(hd), cache_k, cache_v)
    lru_o, h_s = _lru_step(x_lru, y_lru, jnp.swapaxes(state_conv, 0, 1), state_rglru, *lru_p)
    y_sample = finish(xs, attn_s.reshape(bd, attn_w), lru_o, ga, gb).reshape(bd, dec_seq, d_model)
    conv_s = jnp.concatenate([state_conv[:, 1:], x_lru[:, None, :]], axis=1)

    return (y_prompt, y_sample,
            k_p.reshape(bp, seq, n_heads, head_dim), v_p.reshape(bp, seq, n_heads, head_dim),
            ki_p.reshape(bp, seq, idx_dim), conv_p, h_p.reshape(bp, lru_w),
            k_s.reshape(bd, dec_seq, n_heads, head_dim), v_s.reshape(bd, dec_seq, n_heads, head_dim),
            ki_s.reshape(bd, dec_seq, idx_dim), conv_s, h_s)
```
